```python
import math
import jax
import jax.numpy as jnp
from jax import lax
import numpy as np

D_MODEL = 1024
BATCH = 4
SEQ = 8192
DEPTH = 1

MIX_WIDTH = D_MODEL
DIFF_QK_DIM = 64
DIFF_V_DIM = 2 * DIFF_QK_DIM
WIDTH_DIFF = MIX_WIDTH // 2
N_HEADS_DIFF = WIDTH_DIFF // DIFF_V_DIM
HGRN_DIM = 128
WIDTH_HGRN = MIX_WIDTH - WIDTH_DIFF
N_HEADS_HGRN = WIDTH_HGRN // HGRN_DIM
IN_COLS = 3 * WIDTH_DIFF + 5 * WIDTH_HGRN
REL_BUCKETS = 32
REL_MAX_DIST = 128
Q_BLOCK = 128
CHUNK = 64
D_FF = 4 * D_MODEL
CONV_WIDTH = 3
PLE_DIM = 256
EPS = 1e-6

kernel_name = 'hybrid_diffattn_hgrn2_encoder_layer'


def rms_norm(x, g):
    xf = x.astype(jnp.float32)
    y = xf * lax.rsqrt(jnp.mean(xf * xf, axis=-1, keepdims=True) + EPS)
    return (y * g.astype(jnp.float32)).astype(x.dtype)


def t5_bucket(rel):
    half = REL_BUCKETS // 2
    max_exact = half // 2
    n = jnp.abs(rel)
    scaled = jnp.log(jnp.maximum(n, 1).astype(jnp.float32) / max_exact) / math.log(REL_MAX_DIST / max_exact)
    large = jnp.minimum(max_exact + (scaled * (half - max_exact)).astype(jnp.int32), half - 1)
    return jnp.where(rel > 0, half, 0) + jnp.where(n < max_exact, n, large)


def diff_attention(q, k, v, rel_bias, lam):
    b, t = q.shape[0], q.shape[1]
    nblk = t // Q_BLOCK
    q = q * (DIFF_QK_DIM ** -0.5)
    qb = jnp.moveaxis(q.reshape(b, nblk, Q_BLOCK, *q.shape[2:]), 1, 0)
    starts = jnp.arange(nblk, dtype=jnp.int32) * Q_BLOCK
    kpos = jnp.arange(t, dtype=jnp.int32)

    def block(args):
        qblk, start = args
        qpos = start + jnp.arange(Q_BLOCK, dtype=jnp.int32)
        bias = jnp.moveaxis(rel_bias[t5_bucket(kpos[None, :] - qpos[:, None])], -1, 0).astype(jnp.float32)
        logits = jnp.einsum('bqhcd,bkhcd->bhcqk', qblk, k, preferred_element_type=jnp.float32) + bias[None, :, None]
        probs = jax.nn.softmax(logits, axis=-1)
        attn = probs[:, :, 0] - lam * probs[:, :, 1]
        return jnp.einsum('bhqk,bkhd->bqhd', attn.astype(v.dtype), v)

    out = lax.map(block, (qb, starts))
    return jnp.moveaxis(out, 0, 1).reshape(b, t, *v.shape[2:])


def hgrn2_bidirectional(q, i, f_fwd, f_bwd, lb):
    b, t, h, d = q.shape
    nc = t // CHUNK
    rev = lambda a: jnp.flip(a, axis=1)
    qf = jax.nn.silu(q.astype(jnp.float32))
    qs = jnp.stack([qf, rev(qf)])
    vs = jnp.stack([i, rev(i)]).astype(jnp.float32)
    lbb = lb.astype(jnp.float32)[:, None, None]
    f = lbb + (1.0 - lbb) * jax.nn.sigmoid(jnp.stack([f_fwd, rev(f_bwd)]).astype(jnp.float32))
    ks = 1.0 - f
    logf = jnp.log(f)

    def to_chunks(a):
        return a.reshape(2, b, nc, CHUNK, h, d).transpose(2, 0, 1, 4, 3, 5)

    tri = jnp.tril(jnp.ones((CHUNK, CHUNK), dtype=bool))[:, :, None]

    def step(S, xs):
        qc, kc, lfc, vc = xs
        cum = jnp.cumsum(lfc, axis=-2)
        inter = jnp.einsum('nbhtk,nbhkv->nbhtv', qc * jnp.exp(cum), S)
        rel = jnp.where(tri, cum[..., :, None, :] - cum[..., None, :, :], -jnp.inf)
        scores = jnp.einsum('nbhtk,nbhsk,nbhtsk->nbhts', qc, kc, jnp.exp(rel))
        intra = jnp.einsum('nbhts,nbhsv->nbhtv', scores, vc)
        last = cum[..., -1:, :]
        S = jnp.exp(last[..., 0, :])[..., None] * S + jnp.einsum('nbhsk,nbhsv->nbhkv', kc * jnp.exp(last - cum), vc)
        return S, inter + intra

    S0 = jnp.zeros((2, b, h, d, d), jnp.float32)
    _, o = lax.scan(step, S0, (to_chunks(qs), to_chunks(ks), to_chunks(logf), to_chunks(vs)))
    o = o.transpose(1, 2, 0, 4, 3, 5).reshape(2, b, t, h, d)
    return o[0] + rev(o[1])


def conv_ffn(u, w_up, conv_w, conv_b, w_down):
    a = u @ w_up
    a = lax.conv_general_dilated(a, conv_w[:, None, :].astype(a.dtype), window_strides=(1,),
                                 padding=((CONV_WIDTH // 2, CONV_WIDTH // 2),),
                                 dimension_numbers=('NWC', 'WIO', 'NWC'),
                                 feature_group_count=a.shape[-1]) + conv_b.astype(a.dtype)
    gate, up = jnp.split(a, 2, axis=-1)
    return (jax.nn.gelu(gate, approximate=True) * up) @ w_down


def setup_inputs(seed: int = 0) -> dict:
    key = jax.random.key(seed)
    ks = jax.random.split(key, 24)
    f32 = jnp.float32

    def nrm(k, shape, scale):
        return jax.random.normal(k, shape, f32) * scale

    def gain(k, shape):
        return 1.0 + 0.05 * jax.random.normal(k, shape, f32)

    return {
        'x': nrm(ks[0], (BATCH, SEQ, D_MODEL), 1.0),
        'p': nrm(ks[1], (DEPTH, BATCH, SEQ, PLE_DIM), 1.0),
        'rel_bias': nrm(ks[2], (REL_BUCKETS, N_HEADS_DIFF), 0.5),
        'g_pre_mix': gain(ks[3], (DEPTH, D_MODEL)),
        'w_in': nrm(ks[4], (DEPTH, D_MODEL, IN_COLS), D_MODEL ** -0.5),
        'lambda_q1': nrm(ks[5], (DEPTH, DIFF_QK_DIM), 0.1),
        'lambda_k1': nrm(ks[6], (DEPTH, DIFF_QK_DIM), 0.1),
        'lambda_q2': nrm(ks[7], (DEPTH, DIFF_QK_DIM), 0.1),
        'lambda_k2': nrm(ks[8], (DEPTH, DIFF_QK_DIM), 0.1),
        'g_diff': gain(ks[9], (DEPTH, DIFF_V_DIM)),
        'lb_param': nrm(ks[10], (2, DEPTH + 1, WIDTH_HGRN), 1.0),
        'g_hgrn': gain(ks[11], (DEPTH, HGRN_DIM)),
        'w_out': nrm(ks[12], (DEPTH, MIX_WIDTH, D_MODEL), MIX_WIDTH ** -0.5),
        'g_post_mix': gain(ks[13], (DEPTH, D_MODEL)),
        'g_pre_ffn': gain(ks[14], (DEPTH, D_MODEL)),
        'w_up': nrm(ks[15], (DEPTH, D_MODEL, 2 * D_FF), D_MODEL ** -0.5),
        'conv_w': nrm(ks[16], (DEPTH, CONV_WIDTH, 2 * D_FF), CONV_WIDTH ** -0.5),
        'conv_b': nrm(ks[17], (DEPTH, 2 * D_FF), 0.01),
        'w_down': nrm(ks[18], (DEPTH, D_FF, D_MODEL), D_FF ** -0.5),
        'g_post_ffn': gain(ks[19], (DEPTH, D_MODEL)),
        'w_ple': nrm(ks[20], (DEPTH, PLE_DIM, D_MODEL), PLE_DIM ** -0.5),
        'g_ple': gain(ks[21], (DEPTH, D_MODEL)),
        'w_ple_gate': nrm(ks[22], (DEPTH, D_MODEL, D_MODEL), D_MODEL ** -0.5),
    }


def reference(x, p, rel_bias, g_pre_mix, w_in, lambda_q1, lambda_k1, lambda_q2, lambda_k2, g_diff,
              lb_param, g_hgrn, w_out, g_post_mix, g_pre_ffn, w_up, conv_w, conv_b, w_down,
              g_post_ffn, w_ple, g_ple, w_ple_gate):
    f32 = jnp.float32
    b, t, _ = x.shape
    wd, wh = WIDTH_DIFF, WIDTH_HGRN
    splits = [wd, 2 * wd, 3 * wd, 3 * wd + wh, 3 * wd + 2 * wh, 3 * wd + 3 * wh, 3 * wd + 4 * wh]
    lower_bounds = jnp.cumsum(jax.nn.softmax(lb_param.astype(f32), axis=1), axis=1)
    h = x
    for l in range(DEPTH):
        u = rms_norm(h, g_pre_mix[l])
        proj = u @ w_in[l]
        q_d, k_d, v_d, q_h, i_h, f_fw, f_bw, g_h = jnp.split(proj, splits, axis=-1)

        lam_init = 0.8 - 0.6 * math.exp(-0.3 * l)
        lam = (jnp.exp(jnp.sum(lambda_q1[l].astype(f32) * lambda_k1[l].astype(f32)))
               - jnp.exp(jnp.sum(lambda_q2[l].astype(f32) * lambda_k2[l].astype(f32))) + lam_init)
        qk_shape = (b, t, N_HEADS_DIFF, 2, DIFF_QK_DIM)
        a = diff_attention(q_d.reshape(qk_shape), k_d.reshape(qk_shape),
                           v_d.reshape(b, t, N_HEADS_DIFF, DIFF_V_DIM), rel_bias, lam)
        a = (rms_norm(a, g_diff[l]) * (1.0 - lam_init)).astype(x.dtype).reshape(b, t, WIDTH_DIFF)

        hd = (b, t, N_HEADS_HGRN, HGRN_DIM)
        lb = lower_bounds[:, l].reshape(2, N_HEADS_HGRN, HGRN_DIM)
        o = hgrn2_bidirectional(q_h.reshape(hd), i_h.reshape(hd), f_fw.reshape(hd), f_bw.reshape(hd), lb)
        o = (rms_norm(o, g_hgrn[l]) * jax.nn.silu(g_h.reshape(hd).astype(f32))).astype(x.dtype).reshape(b, t, WIDTH_HGRN)

        m = jnp.concatenate([a, o], axis=-1) @ w_out[l]
        h = h + rms_norm(m, g_post_mix[l])

        y = conv_ffn(rms_norm(h, g_pre_ffn[l]), w_up[l], conv_w[l], conv_b[l], w_down[l])
        h = h + rms_norm(y, g_post_ffn[l])

        e = rms_norm(p[l] @ w_ple[l], g_ple[l]) * jax.nn.sigmoid(h @ w_ple_gate[l])
        h = h + e
    return h
```

```python
import functools
import math

import jax
import jax.numpy as jnp
from jax import lax
from jax.experimental import pallas as pl
from jax.experimental.pallas import tpu as pltpu

F32 = jnp.float32
BF16 = jnp.bfloat16

EPS = 1e-6
DIFF_QK_DIM = 64
HEAD_DIM = 128
REL_BUCKETS = 32
REL_MAX_DIST = 128
CONV_WIDTH = 3

ATT_BLOCK = 512
ATT_KSUB = 4
HGRN_CHUNK = 128
HGRN_BASE = 16
HGRN_STEP = 512
TOKEN_TILE = 512
FFN_TOKEN_TILE = 1024
FFN_COL_TILE = 512
HALO_ROWS = 16
VMEM_LIMIT = 56 * 1024 * 1024


def _cparams(sem):
    return pltpu.CompilerParams(dimension_semantics=sem, vmem_limit_bytes=VMEM_LIMIT)


def _rms(x, g):
    return x * lax.rsqrt(jnp.mean(x * x, axis=-1, keepdims=True) + EPS) * g


def _sigmoid(x):
    return 1.0 / (1.0 + jnp.exp(-x))


def _dot(a, b):
    return jnp.dot(a, b, preferred_element_type=F32)


def _dot_nt(a, b):
    return lax.dot_general(a, b, (((1,), (1,)), ((), ())), preferred_element_type=F32)


def _log_bucket_thresholds():
    half = REL_BUCKETS // 2
    max_exact = half // 2
    nlog = half - max_exact
    thr = []
    for k in range(1, nlog):
        n = max_exact
        while (n ** nlog) * (max_exact ** k) < (REL_MAX_DIST ** k) * (max_exact ** nlog):
            n += 1
        thr.append(n)
    return max_exact, half, thr


def _bias_kernel(rb_ref, o_ref, *, blk):
    h = pl.program_id(0)
    d = pl.program_id(1) - 2
    max_exact, half, thr = _log_bucket_thresholds()
    row = lax.broadcasted_iota(jnp.int32, (blk, blk), 0)
    col = lax.broadcasted_iota(jnp.int32, (blk, blk), 1)
    rel = d * blk + col - row
    n = jnp.abs(rel)
    large = jnp.full((blk, blk), max_exact, jnp.int32)
    for t in thr:
        large = large + (n >= t).astype(jnp.int32)
    bucket = jnp.where(rel > 0, half, 0) + jnp.where(n < max_exact, n, large)
    bias = jnp.zeros((blk, blk), F32)
    for b in range(REL_BUCKETS):
        bias = jnp.where(bucket == b, rb_ref[b, h], bias)
    o_ref[0, 0] = bias


def _rel_bias_blocks(rel_bias, blk):
    assert blk >= REL_MAX_DIST
    nh = rel_bias.shape[1]
    return pl.pallas_call(
        functools.partial(_bias_kernel, blk=blk),
        grid=(nh, 5),
        in_specs=[pl.BlockSpec(memory_space=pltpu.SMEM)],
        out_specs=pl.BlockSpec((1, 1, blk, blk), lambda h, d: (h, d, 0, 0)),
        out_shape=jax.ShapeDtypeStruct((nh, 5, blk, blk), F32),
        compiler_params=_cparams(("parallel", "parallel")),
        name="rel_bias_blocks",
    )(rel_bias)


def _in_proj_kernel(x_ref, g_ref, w_ref, qkv_ref, hq_ref, hi_ref, hit_ref, hf_ref, hg_ref, *, wd, wh):
    u = _rms(x_ref[0], g_ref[...]).astype(BF16)

    def proj(a, b):
        return _dot(u, w_ref[:, a:b])

    qkv_ref[0, :, 0:wd] = (proj(0, wd) * (DIFF_QK_DIM ** -0.5)).astype(BF16)
    qkv_ref[0, :, wd:3 * wd] = proj(wd, 3 * wd).astype(BF16)
    o = 3 * wd
    qh = proj(o, o + wh)
    hq_ref[0] = (qh * _sigmoid(qh)).astype(BF16)
    ih = proj(o + wh, o + 2 * wh)
    hi_ref[0] = ih.astype(BF16)
    hit_ref[0] = ih.T.astype(BF16)
    hf_ref[0] = proj(o + 2 * wh, o + 4 * wh)
    gh = proj(o + 4 * wh, o + 5 * wh)
    hg_ref[0] = gh * _sigmoid(gh)


def _in_proj(x, g, w, wd, wh):
    b, t, d = x.shape
    tm = TOKEN_TILE
    ncol = w.shape[1]
    row = lambda bi, ti: (bi, ti, 0)
    return pl.pallas_call(
        functools.partial(_in_proj_kernel, wd=wd, wh=wh),
        grid=(b, t // tm),
        in_specs=[pl.BlockSpec((1, tm, d), row),
                  pl.BlockSpec((1, d), lambda bi, ti: (0, 0)),
                  pl.BlockSpec((d, ncol), lambda bi, ti: (0, 0))],
        out_specs=[pl.BlockSpec((1, tm, 3 * wd), row),
                   pl.BlockSpec((1, tm, wh), row),
                   pl.BlockSpec((1, tm, wh), row),
                   pl.BlockSpec((1, wh, tm), lambda bi, ti: (bi, 0, ti)),
                   pl.BlockSpec((1, tm, 2 * wh), row),
                   pl.BlockSpec((1, tm, wh), row)],
        out_shape=[jax.ShapeDtypeStruct((b, t, 3 * wd), BF16),
                   jax.ShapeDtypeStruct((b, t, wh), BF16),
                   jax.ShapeDtypeStruct((b, t, wh), BF16),
                   jax.ShapeDtypeStruct((b, wh, t), BF16),
                   jax.ShapeDtypeStruct((b, t, 2 * wh), F32),
                   jax.ShapeDtypeStruct((b, t, wh), F32)],
        compiler_params=_cparams(("parallel", "parallel")),
        name="in_proj",
    )(x, g, w)


def _attn_kernel(q_ref, k_ref, v_ref, bias_ref, lq1_ref, lk1_ref, lq2_ref, lk2_ref, gd_ref,
                 o_ref, m_ref, l_ref, acc_ref, *, blk, ksub, seq, lam_init):
    qi = pl.program_id(2)
    q = q_ref[0]
    lane = lax.broadcasted_iota(jnp.int32, q.shape, 1)
    zero = jnp.zeros_like(q)
    qq = jnp.concatenate([jnp.where(lane < DIFF_QK_DIM, q, zero),
                          jnp.where(lane >= DIFF_QK_DIM, q, zero)], axis=0)
    m_ref[...] = jnp.full(m_ref.shape, -jnp.inf, F32)
    l_ref[...] = jnp.zeros(l_ref.shape, F32)
    acc_ref[...] = jnp.zeros(acc_ref.shape, F32)
    tk = blk * ksub

    def body(j, carry):
        parts = []
        for u in range(ksub):
            kb = j * ksub + u
            kc = k_ref[0, pl.ds(pl.multiple_of(kb * blk, blk), blk), :]
            s = _dot_nt(qq, kc)
            bias = bias_ref[0, jnp.clip(kb - qi, -2, 2) + 2]
            parts.append(s + jnp.concatenate([bias, bias], axis=0))
        s = jnp.concatenate(parts, axis=1)
        m_prev = m_ref[...]
        m_new = jnp.maximum(m_prev, jnp.max(s, axis=1, keepdims=True))
        alpha = jnp.exp(m_prev - m_new)
        p = jnp.exp(s - m_new)
        l_ref[...] = alpha * l_ref[...] + jnp.sum(p, axis=1, keepdims=True)
        vc = v_ref[0, pl.ds(pl.multiple_of(j * tk, tk), tk), :]
        acc_ref[...] = alpha * acc_ref[...] + _dot(p.astype(BF16), vc)
        m_ref[...] = m_new
        return carry

    lax.fori_loop(0, seq // tk, body, 0)

    lam = (jnp.exp(jnp.sum(lq1_ref[...] * lk1_ref[...])) - jnp.exp(jnp.sum(lq2_ref[...] * lk2_ref[...]))
           + lam_init)
    o_all = acc_ref[...] / l_ref[...]
    o = o_all[:blk] - lam * o_all[blk:]
    o_ref[0] = (_rms(o, gd_ref[...]) * (1.0 - lam_init)).astype(BF16)


def _diff_attn(qkv, bias, lq1, lk1, lq2, lk2, g_diff, nh, lam_init):
    b, t, _ = qkv.shape
    blk = ATT_BLOCK
    ksub = min(ATT_KSUB, t // blk)
    small = lambda bi, h, qi: (0, 0)
    return pl.pallas_call(
        functools.partial(_attn_kernel, blk=blk, ksub=ksub, seq=t, lam_init=lam_init),
        grid=(b, nh, t // blk),
        in_specs=[pl.BlockSpec((1, blk, HEAD_DIM), lambda bi, h, qi: (bi, qi, h)),
                  pl.BlockSpec((1, t, HEAD_DIM), lambda bi, h, qi: (bi, 0, nh + h)),
                  pl.BlockSpec((1, t, HEAD_DIM), lambda bi, h, qi: (bi, 0, 2 * nh + h)),
                  pl.BlockSpec((1, 5, blk, blk), lambda bi, h, qi: (h, 0, 0, 0)),
                  pl.BlockSpec((1, DIFF_QK_DIM), small), pl.BlockSpec((1, DIFF_QK_DIM), small),
                  pl.BlockSpec((1, DIFF_QK_DIM), small), pl.BlockSpec((1, DIFF_QK_DIM), small),
                  pl.BlockSpec((1, HEAD_DIM), small)],
        out_specs=pl.BlockSpec((1, blk, HEAD_DIM), lambda bi, h, qi: (bi, qi, h)),
        out_shape=jax.ShapeDtypeStruct((b, t, nh * HEAD_DIM), BF16),
        scratch_shapes=[pltpu.VMEM((2 * blk, 1), F32), pltpu.VMEM((2 * blk, 1), F32),
                        pltpu.VMEM((2 * blk, HEAD_DIM), F32)],
        compiler_params=_cparams(("parallel", "parallel", "arbitrary")),
        name="diff_attn",
    )(qkv, qkv, qkv, bias, lq1, lk1, lq2, lk2, g_diff)


def _hgrn_masks(rev):
    c = HGRN_CHUNK
    row = lax.broadcasted_iota(jnp.int32, (c, c), 0)
    col = lax.broadcasted_iota(jnp.int32, (c, c), 1)
    tri = (col >= row) if rev else (col <= row)
    masks = []
    b = HGRN_BASE
    while b < c:
        masks.append((b, ((row ^ col) >> int(math.log2(b))) == 1))
        b *= 2
    sh = int(math.log2(HGRN_BASE))
    diag = ((row >> sh) == (col >> sh)) & tri
    return tri.astype(BF16), masks, diag


def _row_refs(cum, idxs, rows_each):
    return jnp.concatenate(
        [jnp.broadcast_to(cum[i:i + 1, :], (rows_each, cum.shape[1])) for i in idxs], axis=0)


def _hgrn_chunk(q, kk, logf, v, vt, st, tri, masks, diag, rev):
    c = HGRN_CHUNK
    hi = logf.astype(BF16)
    r1 = logf - hi.astype(F32)
    mid = r1.astype(BF16)
    lo = (r1 - mid.astype(F32)).astype(BF16)
    cum = _dot(tri, hi) + _dot(tri, mid) + _dot(tri, lo)
    last = cum[0:1, :] if rev else cum[c - 1:c, :]
    rowi = lax.broadcasted_iota(jnp.int32, (c, q.shape[1]), 0)

    a = jnp.zeros((c, c), F32)
    for b, same_group in masks:
        ngroups = c // (2 * b)
        ref = _row_refs(cum, [g * 2 * b + (b if rev else b - 1) for g in range(ngroups)], 2 * b)
        first_half = (rowi & (2 * b - 1)) < b
        is_q = first_half if rev else jnp.logical_not(first_half)
        qt = jnp.where(is_q, q * jnp.exp(jnp.minimum(cum - ref, 0.0)), 0.0)
        kt = jnp.where(is_q, 0.0, kk * jnp.exp(jnp.minimum(ref - cum, 0.0)))
        a = jnp.where(same_group, _dot_nt(qt.astype(BF16), kt.astype(BF16)), a)
    nb = c // HGRN_BASE
    ref = _row_refs(cum, [g * HGRN_BASE + HGRN_BASE // 2 for g in range(nb)], HGRN_BASE)
    qd = q * jnp.exp(cum - ref)
    kd = kk * jnp.exp(ref - cum)
    a = jnp.where(diag, _dot_nt(qd.astype(BF16), kd.astype(BF16)), a)

    inter = _dot_nt((q * jnp.exp(cum)).astype(BF16), st.astype(BF16))
    out = inter + _dot(a.astype(BF16), v)
    kl = (kk * jnp.exp(last - cum)).astype(BF16)
    st_new = st * jnp.exp(last) + _dot(vt, kl)
    return out, st_new


def _hgrn_kernel(*refs, rev, nchunk, level):
    if rev:
        q_ref, v_ref, vt_ref, f_ref, lb_ref, of_ref, g_ref, gn_ref, o_ref, st_ref = refs
    else:
        q_ref, v_ref, vt_ref, f_ref, lb_ref, o_ref, st_ref = refs

    @pl.when(pl.program_id(2) == 0)
    def _():
        st_ref[...] = jnp.zeros(st_ref.shape, F32)

    lp = lb_ref[0]
    e = jnp.exp(lp - jnp.max(lp, axis=0, keepdims=True))
    lb = jnp.sum(e[0:level + 1], axis=0, keepdims=True) / jnp.sum(e, axis=0, keepdims=True)

    tri, masks, diag = _hgrn_masks(rev)
    c = HGRN_CHUNK
    st = st_ref[...]
    order = range(nchunk - 1, -1, -1) if rev else range(nchunk)
    for ci in order:
        rows = slice(ci * c, (ci + 1) * c)
        f = lb + (1.0 - lb) * _sigmoid(f_ref[0, rows, :])
        out, st = _hgrn_chunk(q_ref[0, rows, :].astype(F32), 1.0 - f, jnp.log(f), v_ref[0, rows, :],
                              vt_ref[0, :, rows], st, tri, masks, diag, rev)
        if rev:
            o = out + of_ref[0, rows, :]
            o_ref[0, rows, :] = (_rms(o, gn_ref[...]) * g_ref[0, rows, :]).astype(BF16)
        else:
            o_ref[0, rows, :] = out
    st_ref[...] = st


def _hgrn_dir(hq, hi, hit, hf, lb_param, nh, rev, level, extra=()):
    b, t, _ = hq.shape
    step = min(HGRN_STEP, t)
    nstep = t // step
    d = HEAD_DIM
    tpos = (lambda n: nstep - 1 - n) if rev else (lambda n: n)
    row = lambda bi, h, n: (bi, tpos(n), h)
    in_specs = [pl.BlockSpec((1, step, d), row),
                pl.BlockSpec((1, step, d), row),
                pl.BlockSpec((1, d, step), lambda bi, h, n: (bi, h, tpos(n))),
                pl.BlockSpec((1, step, d), lambda bi, h, n: (bi, tpos(n), (nh if rev else 0) + h)),
                pl.BlockSpec((1, lb_param.shape[1], d), lambda bi, h, n: (1 if rev else 0, 0, h))]
    if rev:
        in_specs += [pl.BlockSpec((1, step, d), row), pl.BlockSpec((1, step, d), row),
                     pl.BlockSpec((1, d), lambda bi, h, n: (0, 0))]
    return pl.pallas_call(
        functools.partial(_hgrn_kernel, rev=rev, nchunk=step // HGRN_CHUNK, level=level),
        grid=(b, nh, nstep),
        in_specs=in_specs,
        out_specs=pl.BlockSpec((1, step, d), row),
        out_shape=jax.ShapeDtypeStruct((b, t, nh * d), BF16 if rev else F32),
        scratch_shapes=[pltpu.VMEM((d, d), F32)],
        compiler_params=_cparams(("parallel", "parallel", "arbitrary")),
        name="hgrn_bwd" if rev else "hgrn_fwd",
    )(hq, hi, hit, hf, lb_param, *extra)


def _out_proj_kernel(a_ref, o_ref, x_ref, w_ref, gp_ref, gf_ref, h_ref, u_ref, *, wd):
    m = _dot(a_ref[...], w_ref[0:wd, :]) + _dot(o_ref[...], w_ref[wd:, :])
    h = x_ref[...] + _rms(m, gp_ref[...])
    h_ref[...] = h
    u_ref[...] = _rms(h, gf_ref[...]).astype(BF16)


def _out_proj(a, o, x, w, g_post, g_ffn):
    n, d = x.shape
    wd = a.shape[1]
    tm = TOKEN_TILE
    row = lambda i: (i, 0)
    fixed = lambda i: (0, 0)
    return pl.pallas_call(
        functools.partial(_out_proj_kernel, wd=wd),
        grid=(n // tm,),
        in_specs=[pl.BlockSpec((tm, wd), row), pl.BlockSpec((tm, o.shape[1]), row), pl.BlockSpec((tm, d), row),
                  pl.BlockSpec(w.shape, fixed), pl.BlockSpec((1, d), fixed), pl.BlockSpec((1, d), fixed)],
        out_specs=[pl.BlockSpec((tm, d), row), pl.BlockSpec((tm, d), row)],
        out_shape=[jax.ShapeDtypeStruct((n, d), F32), jax.ShapeDtypeStruct((n, d), BF16)],
        compiler_params=_cparams(("parallel",)),
        name="out_proj",
    )(a, o, x, w, g_post, g_ffn)


def _ffn_kernel(u_ref, up_ref, un_ref, wg_ref, wu_ref, cwg_ref, cwu_ref, cbg_ref, cbu_ref, wd_ref,
                h_ref, g_ref, o_ref, acc_ref, *, tiles_per_seq):
    i = pl.program_id(0)
    j = pl.program_id(1)
    nj = pl.num_programs(1)
    tm = u_ref.shape[0]

    @pl.when(j == 0)
    def _():
        acc_ref[...] = jnp.zeros(acc_ref.shape, F32)

    u = u_ref[...]
    halo = jnp.concatenate([up_ref[...], un_ref[...]], axis=0)
    ti = i % tiles_per_seq
    has_prev = jnp.where(ti > 0, 1.0, 0.0)
    has_next = jnp.where(ti < tiles_per_seq - 1, 1.0, 0.0)
    rowi = lax.broadcasted_iota(jnp.int32, (tm, wg_ref.shape[1]), 0)

    def branch(w_ref, cw_ref, cb_ref):
        w = w_ref[...]
        a = _dot(u, w)
        ah = _dot(halo, w)
        prev_row = ah[HALO_ROWS - 1:HALO_ROWS, :] * has_prev
        next_row = ah[HALO_ROWS:HALO_ROWS + 1, :] * has_next
        a_prev = jnp.where(rowi == 0, prev_row, pltpu.roll(a, 1, 0))
        a_next = jnp.where(rowi == tm - 1, next_row, pltpu.roll(a, tm - 1, 0))
        cw = cw_ref[...]
        return cw[0:1, :] * a_prev + cw[1:2, :] * a + cw[2:3, :] * a_next + cb_ref[...]

    gate = branch(wg_ref, cwg_ref, cbg_ref)
    up = branch(wu_ref, cwu_ref, cbu_ref)
    c0 = math.sqrt(2.0 / math.pi)
    gelu = 0.5 * gate * (1.0 + jnp.tanh(c0 * (gate + 0.044715 * (gate * gate * gate))))
    acc_ref[...] += _dot((gelu * up).astype(BF16), wd_ref[...])

    @pl.when(j == nj - 1)
    def _():
        o_ref[...] = h_ref[...] + _rms(acc_ref[...], g_ref[...])


def _conv_ffn(u, h, w_up, conv_w, conv_b, w_down, g_post, seq):
    n, d = h.shape
    dff = w_down.shape[0]
    tm = min(FFN_TOKEN_TILE, seq)
    tf = FFN_COL_TILE
    nf = dff // tf
    hb = tm // HALO_ROWS
    nhalo = n // HALO_ROWS
    row = lambda i, j: (i, 0)
    return pl.pallas_call(
        functools.partial(_ffn_kernel, tiles_per_seq=seq // tm),
        grid=(n // tm, nf),
        in_specs=[pl.BlockSpec((tm, d), row),
                  pl.BlockSpec((HALO_ROWS, d), lambda i, j: (jnp.maximum(i * hb - 1, 0), 0)),
                  pl.BlockSpec((HALO_ROWS, d), lambda i, j: (jnp.minimum((i + 1) * hb, nhalo - 1), 0)),
                  pl.BlockSpec((d, tf), lambda i, j: (0, j)),
                  pl.BlockSpec((d, tf), lambda i, j: (0, nf + j)),
                  pl.BlockSpec((CONV_WIDTH, tf), lambda i, j: (0, j)),
                  pl.BlockSpec((CONV_WIDTH, tf), lambda i, j: (0, nf + j)),
                  pl.BlockSpec((1, tf), lambda i, j: (0, j)),
                  pl.BlockSpec((1, tf), lambda i, j: (0, nf + j)),
                  pl.BlockSpec((tf, d), lambda i, j: (j, 0)),
                  pl.BlockSpec((tm, d), row),
                  pl.BlockSpec((1, d), lambda i, j: (0, 0))],
        out_specs=pl.BlockSpec((tm, d), row),
        out_shape=jax.ShapeDtypeStruct((n, d), F32),
        scratch_shapes=[pltpu.VMEM((tm, d), F32)],
        compiler_params=_cparams(("parallel", "arbitrary")),
        name="conv_ffn",
    )(u, u, u, w_up, w_up, conv_w, conv_w, conv_b, conv_b, w_down, h, g_post)


def _ple_kernel(h_ref, p_ref, wp_ref, wg_ref, g_ref, o_ref):
    h = h_ref[...]
    e = _rms(_dot(p_ref[...].astype(BF16), wp_ref[...]), g_ref[...])
    gate = _sigmoid(_dot(h.astype(BF16), wg_ref[...]))
    o_ref[...] = h + e * gate


def _ple(h, p, w_ple, w_gate, g):
    n, d = h.shape
    tm = TOKEN_TILE
    row = lambda i: (i, 0)
    fixed = lambda i: (0, 0)
    return pl.pallas_call(
        _ple_kernel,
        grid=(n // tm,),
        in_specs=[pl.BlockSpec((tm, d), row), pl.BlockSpec((tm, p.shape[1]), row),
                  pl.BlockSpec(w_ple.shape, fixed), pl.BlockSpec(w_gate.shape, fixed), pl.BlockSpec((1, d), fixed)],
        out_specs=pl.BlockSpec((tm, d), row),
        out_shape=jax.ShapeDtypeStruct((n, d), F32),
        compiler_params=_cparams(("parallel",)),
        name="ple",
    )(h, p, w_ple, w_gate, g)


def kernel(x, p, rel_bias, g_pre_mix, w_in, lambda_q1, lambda_k1, lambda_q2, lambda_k2, g_diff, lb_param, g_hgrn, w_out, g_post_mix, g_pre_ffn, w_up, conv_w, conv_b, w_down, g_post_ffn, w_ple, g_ple, w_ple_gate):
    b, t, d = x.shape
    depth = w_in.shape[0]
    wd = d // 2
    wh = d - wd
    nh_diff = wd // HEAD_DIM
    nh_hgrn = wh // HEAD_DIM
    assert w_in.shape[2] == 3 * wd + 5 * wh
    assert t % ATT_BLOCK == 0 and t % HGRN_CHUNK == 0 and (b * t) % TOKEN_TILE == 0

    bias = _rel_bias_blocks(rel_bias.astype(F32), ATT_BLOCK)
    h = x
    for l in range(depth):
        vec = lambda a: a[l].reshape(1, -1).astype(F32)
        hid = h
        qkv, hq, hi, hit, hf, hg = _in_proj(hid, vec(g_pre_mix), w_in[l].astype(BF16), wd, wh)
        lam_init = 0.8 - 0.6 * math.exp(-0.3 * l)
        a = _diff_attn(qkv, bias, vec(lambda_q1), vec(lambda_k1), vec(lambda_q2), vec(lambda_k2),
                       vec(g_diff), nh_diff, lam_init)
        lbp = lb_param.astype(F32)
        o_fwd = _hgrn_dir(hq, hi, hit, hf, lbp, nh_hgrn, False, l)
        o = _hgrn_dir(hq, hi, hit, hf, lbp, nh_hgrn, True, l, extra=(o_fwd, hg, vec(g_hgrn)))
        n = b * t
        h1, u2 = _out_proj(a.reshape(n, wd), o.reshape(n, wh), hid.reshape(n, d), w_out[l].astype(BF16),
                           vec(g_post_mix), vec(g_pre_ffn))
        h2 = _conv_ffn(u2, h1, w_up[l].astype(BF16), conv_w[l].astype(F32), conv_b[l].reshape(1, -1).astype(F32),
                       w_down[l].astype(BF16), vec(g_post_ffn), t)
        h3 = _ple(h2, p[l].reshape(n, -1), w_ple[l].astype(BF16), w_ple_gate[l].astype(BF16), vec(g_ple))
        h = h3.reshape(b, t, d)
    return h
```

```python
import functools
import math

import jax
import jax.numpy as jnp
from jax import lax
from jax.experimental import pallas as pl
from jax.experimental.pallas import tpu as pltpu

F32 = jnp.float32
BF16 = jnp.bfloat16

EPS = 1e-6
DIFF_QK_DIM = 64
HEAD_DIM = 128
REL_BUCKETS = 32
REL_MAX_DIST = 128
CONV_WIDTH = 3

ATT_BLOCK = 512
ATT_STRIP = 128
ONES_ROWS = 16
LOG2E = math.log2(math.e)
HGRN_CHUNK = 128
HGRN_BASE = 16
HGRN_STEP = 512
TOKEN_TILE = 512
FFN_TOKEN_TILE = 1024
FFN_COL_TILE = 512
HALO_ROWS = 16
VMEM_LIMIT = 56 * 1024 * 1024


def _cparams(sem, flags=None):
    return pltpu.CompilerParams(dimension_semantics=sem, vmem_limit_bytes=VMEM_LIMIT, flags=flags)


def _rms(x, g):
    return x * lax.rsqrt(jnp.mean(x * x, axis=-1, keepdims=True) + EPS) * g


def _sigmoid(x):
    return 1.0 / (1.0 + jnp.exp(-x))


def _dot(a, b):
    return jnp.dot(a, b, preferred_element_type=F32)


def _dot_nt(a, b):
    return lax.dot_general(a, b, (((1,), (1,)), ((), ())), preferred_element_type=F32)


def _log_bucket_thresholds():
    half = REL_BUCKETS // 2
    max_exact = half // 2
    nlog = half - max_exact
    thr = []
    for k in range(1, nlog):
        n = max_exact
        while (n ** nlog) * (max_exact ** k) < (REL_MAX_DIST ** k) * (max_exact ** nlog):
            n += 1
        thr.append(n)
    return max_exact, half, thr


def _bias_kernel(rb_ref, o_ref, *, blk):
    h = pl.program_id(0)
    d = pl.program_id(1) - 2
    max_exact, half, thr = _log_bucket_thresholds()
    krow = lax.broadcasted_iota(jnp.int32, (blk, blk), 0)
    qcol = lax.broadcasted_iota(jnp.int32, (blk, blk), 1)
    rel = d * blk + krow - qcol
    n = jnp.abs(rel)
    large = jnp.full((blk, blk), max_exact, jnp.int32)
    for t in thr:
        large = large + (n >= t).astype(jnp.int32)
    bucket = jnp.where(rel > 0, half, 0) + jnp.where(n < max_exact, n, large)
    bias = jnp.zeros((blk, blk), F32)
    for b in range(REL_BUCKETS):
        bias = jnp.where(bucket == b, rb_ref[b, h], bias)
    o_ref[0, 0] = bias * LOG2E


def _rel_bias_blocks(rel_bias, blk):
    assert blk >= REL_MAX_DIST
    nh = rel_bias.shape[1]
    return pl.pallas_call(
        functools.partial(_bias_kernel, blk=blk),
        grid=(nh, 5),
        in_specs=[pl.BlockSpec(memory_space=pltpu.SMEM)],
        out_specs=pl.BlockSpec((1, 1, blk, blk), lambda h, d: (h, d, 0, 0)),
        out_shape=jax.ShapeDtypeStruct((nh, 5, blk, blk), F32),
        compiler_params=_cparams(("parallel", "parallel")),
        name="rel_bias_blocks",
    )(rel_bias)


def _in_proj_kernel(x_ref, g_ref, w_ref, qt_ref, k_ref, vt_ref, hq_ref, hi_ref, hit_ref, hf_ref, hg_ref,
                    *, wd, wh):
    u = _rms(x_ref[0], g_ref[...]).astype(BF16)

    def proj(a, b):
        return _dot(u, w_ref[:, a:b])

    qt_ref[0] = (proj(0, wd) * (DIFF_QK_DIM ** -0.5 * LOG2E)).T.astype(BF16)
    k_ref[0] = proj(wd, 2 * wd).astype(BF16)
    vt_ref[0] = proj(2 * wd, 3 * wd).T.astype(BF16)
    o = 3 * wd
    qh = proj(o, o + wh)
    hq_ref[0] = (qh * _sigmoid(qh)).astype(BF16)
    ih = proj(o + wh, o + 2 * wh)
    hi_ref[0] = ih.astype(BF16)
    hit_ref[0] = ih.T.astype(BF16)
    hf_ref[0] = proj(o + 2 * wh, o + 4 * wh)
    gh = proj(o + 4 * wh, o + 5 * wh)
    hg_ref[0] = gh * _sigmoid(gh)


def _in_proj(x, g, w, wd, wh):
    b, t, d = x.shape
    tm = TOKEN_TILE
    ncol = w.shape[1]
    row = lambda bi, ti: (bi, ti, 0)
    col = lambda bi, ti: (bi, 0, ti)
    return pl.pallas_call(
        functools.partial(_in_proj_kernel, wd=wd, wh=wh),
        grid=(b, t // tm),
        in_specs=[pl.BlockSpec((1, tm, d), row),
                  pl.BlockSpec((1, d), lambda bi, ti: (0, 0)),
                  pl.BlockSpec((d, ncol), lambda bi, ti: (0, 0))],
        out_specs=[pl.BlockSpec((1, wd, tm), col),
                   pl.BlockSpec((1, tm, wd), row),
                   pl.BlockSpec((1, wd, tm), col),
                   pl.BlockSpec((1, tm, wh), row),
                   pl.BlockSpec((1, tm, wh), row),
                   pl.BlockSpec((1, wh, tm), col),
                   pl.BlockSpec((1, tm, 2 * wh), row),
                   pl.BlockSpec((1, tm, wh), row)],
        out_shape=[jax.ShapeDtypeStruct((b, wd, t), BF16),
                   jax.ShapeDtypeStruct((b, t, wd), BF16),
                   jax.ShapeDtypeStruct((b, wd, t), BF16),
                   jax.ShapeDtypeStruct((b, t, wh), BF16),
                   jax.ShapeDtypeStruct((b, t, wh), BF16),
                   jax.ShapeDtypeStruct((b, wh, t), BF16),
                   jax.ShapeDtypeStruct((b, t, 2 * wh), F32),
                   jax.ShapeDtypeStruct((b, t, wh), F32)],
        compiler_params=_cparams(("parallel", "parallel")),
        name="in_proj",
    )(x, g, w)


def _attn_kernel(qt_ref, k_ref, vt_ref, bias_ref, lq1_ref, lk1_ref, lq2_ref, lk2_ref, gd_ref,
                 o_ref, m_ref, a_ref, acc_ref, s_ref, p_ref, *, blk, strip, seq, lam_init):
    qi = pl.program_id(2)
    nblk = seq // blk
    qt = qt_ref[0]
    sub = lax.broadcasted_iota(jnp.int32, qt.shape, 0)
    zero = jnp.zeros_like(qt)
    qqt = jnp.concatenate([jnp.where(sub < DIFF_QK_DIM, qt, zero),
                           jnp.where(sub >= DIFF_QK_DIM, qt, zero)], axis=1)
    m_ref[...] = jnp.full(m_ref.shape, -jnp.inf, F32)
    acc_ref[...] = jnp.zeros(acc_ref.shape, F32)
    ones = jnp.ones((ONES_ROWS, blk), BF16)

    nstrip = blk // strip

    def score_strip(kb, slot, r, mx8):
        didx = jnp.clip(kb - qi, -2, 2) + 2
        rows = slice(r * strip, (r + 1) * strip)
        b = bias_ref[0, didx, rows, :]
        kc = k_ref[0, pl.ds(pl.multiple_of(kb * blk, blk) + r * strip, strip), :]
        s = _dot(kc, qqt) + jnp.concatenate([b, b], axis=1)
        s_ref[slot, rows, :] = s
        for i in range(strip // 8):
            mx8 = jnp.maximum(mx8, s[i * 8:(i + 1) * 8])
        return mx8

    def accumulate(kb, slot):
        k0 = pl.multiple_of(kb * blk, blk)
        vta = jnp.concatenate([vt_ref[0, :, pl.ds(k0, blk)], ones], axis=0)
        acc_ref[...] = a_ref[slot] * acc_ref[...] + _dot(vta, p_ref[slot])

    def stage(kb, slot, mx8, do_scores=True, do_values=True):
        m_prev = m_ref[...]
        m_new = jnp.maximum(m_prev, jnp.max(mx8, axis=0, keepdims=True))
        m_ref[...] = m_new
        a_ref[slot] = jnp.exp2(m_prev - m_new)
        nxt = jnp.full((8, 2 * blk), -jnp.inf, F32)
        for r in range(nstrip):
            if do_scores:
                nxt = score_strip(kb + 1, 1 - slot, r, nxt)
            rows = slice(r * strip, (r + 1) * strip)
            p_ref[slot, rows, :] = jnp.exp2(s_ref[slot, rows, :] - m_new).astype(BF16)
        if do_values:
            accumulate(kb - 1, 1 - slot)
        return nxt

    def pair(i, mx8):
        mx8 = stage(2 * i + 1, 1, mx8)
        return stage(2 * i + 2, 0, mx8)

    mx8 = jnp.full((8, 2 * blk), -jnp.inf, F32)
    for r in range(nstrip):
        mx8 = score_strip(0, 0, r, mx8)
    mx8 = stage(0, 0, mx8, do_values=False)
    mx8 = lax.fori_loop(0, nblk // 2 - 1, pair, mx8)
    stage(nblk - 1, 1, mx8, do_scores=False)
    accumulate(nblk - 1, 1)

    lam = (jnp.exp(jnp.sum(lq1_ref[...] * lk1_ref[...])) - jnp.exp(jnp.sum(lq2_ref[...] * lk2_ref[...]))
           + lam_init)
    acc = acc_ref[...]
    ot = acc[:HEAD_DIM] / acc[HEAD_DIM:HEAD_DIM + 1]
    o = (ot[:, :blk] - lam * ot[:, blk:]).T
    o_ref[0] = (_rms(o, gd_ref[...]) * (1.0 - lam_init)).astype(BF16)


def _diff_attn(qt, k, vt, bias, lq1, lk1, lq2, lk2, g_diff, nh, lam_init):
    b, t, _ = k.shape
    blk = ATT_BLOCK
    assert (t // blk) % 2 == 0
    small = lambda bi, h, qi: (0, 0)
    return pl.pallas_call(
        functools.partial(_attn_kernel, blk=blk, strip=ATT_STRIP, seq=t, lam_init=lam_init),
        grid=(b, nh, t // blk),
        in_specs=[pl.BlockSpec((1, HEAD_DIM, blk), lambda bi, h, qi: (bi, h, qi)),
                  pl.BlockSpec((1, t, HEAD_DIM), lambda bi, h, qi: (bi, 0, h)),
                  pl.BlockSpec((1, HEAD_DIM, t), lambda bi, h, qi: (bi, h, 0)),
                  pl.BlockSpec((1, 5, blk, blk), lambda bi, h, qi: (h, 0, 0, 0)),
                  pl.BlockSpec((1, DIFF_QK_DIM), small), pl.BlockSpec((1, DIFF_QK_DIM), small),
                  pl.BlockSpec((1, DIFF_QK_DIM), small), pl.BlockSpec((1, DIFF_QK_DIM), small),
                  pl.BlockSpec((1, HEAD_DIM), small)],
        out_specs=pl.BlockSpec((1, blk, HEAD_DIM), lambda bi, h, qi: (bi, qi, h)),
        out_shape=jax.ShapeDtypeStruct((b, t, nh * HEAD_DIM), BF16),
        scratch_shapes=[pltpu.VMEM((1, 2 * blk), F32),
                        pltpu.VMEM((2, 1, 2 * blk), F32),
                        pltpu.VMEM((HEAD_DIM + ONES_ROWS, 2 * blk), F32),
                        pltpu.VMEM((2, blk, 2 * blk), F32),
                        pltpu.VMEM((2, blk, 2 * blk), BF16)],
        compiler_params=_cparams(("parallel", "parallel", "arbitrary")),
        name="diff_attn",
    )(qt, k, vt, bias, lq1, lk1, lq2, lk2, g_diff)


def _hgrn_masks(rev):
    c = HGRN_CHUNK
    row = lax.broadcasted_iota(jnp.int32, (c, c), 0)
    col = lax.broadcasted_iota(jnp.int32, (c, c), 1)
    tri = (col >= row) if rev else (col <= row)
    masks = []
    b = HGRN_BASE
    while b < c:
        masks.append((b, ((row ^ col) >> int(math.log2(b))) == 1))
        b *= 2
    sh = int(math.log2(HGRN_BASE))
    diag = ((row >> sh) == (col >> sh)) & tri
    return tri.astype(BF16), masks, diag


def _row_refs(cum, idxs, rows_each):
    return jnp.concatenate(
        [jnp.broadcast_to(cum[i:i + 1, :], (rows_each, cum.shape[1])) for i in idxs], axis=0)


def _hgrn_chunk(q, kk, logf, v, vt, st, tri, masks, diag, rev):
    c = HGRN_CHUNK
    hi = logf.astype(BF16)
    r1 = logf - hi.astype(F32)
    mid = r1.astype(BF16)
    lo = (r1 - mid.astype(F32)).astype(BF16)
    cum = _dot(tri, hi) + _dot(tri, mid) + _dot(tri, lo)
    last = cum[0:1, :] if rev else cum[c - 1:c, :]
    rowi = lax.broadcasted_iota(jnp.int32, (c, q.shape[1]), 0)

    a = jnp.zeros((c, c), F32)
    for b, same_group in masks:
        ngroups = c // (2 * b)
        ref = _row_refs(cum, [g * 2 * b + (b if rev else b - 1) for g in range(ngroups)], 2 * b)
        first_half = (rowi & (2 * b - 1)) < b
        is_q = first_half if rev else jnp.logical_not(first_half)
        qt = jnp.where(is_q, q * jnp.exp(jnp.minimum(cum - ref, 0.0)), 0.0)
        kt = jnp.where(is_q, 0.0, kk * jnp.exp(jnp.minimum(ref - cum, 0.0)))
        a = jnp.where(same_group, _dot_nt(qt.astype(BF16), kt.astype(BF16)), a)
    nb = c // HGRN_BASE
    ref = _row_refs(cum, [g * HGRN_BASE + HGRN_BASE // 2 for g in range(nb)], HGRN_BASE)
    qd = q * jnp.exp(cum - ref)
    kd = kk * jnp.exp(ref - cum)
    a = jnp.where(diag, _dot_nt(qd.astype(BF16), kd.astype(BF16)), a)

    inter = _dot_nt((q * jnp.exp(cum)).astype(BF16), st.astype(BF16))
    out = inter + _dot(a.astype(BF16), v)
    kl = (kk * jnp.exp(last - cum)).astype(BF16)
    st_new = st * jnp.exp(last) + _dot(vt, kl)
    return out, st_new


def _hgrn_kernel(*refs, rev, nchunk, level):
    if rev:
        q_ref, v_ref, vt_ref, f_ref, lb_ref, of_ref, g_ref, gn_ref, o_ref, st_ref = refs
    else:
        q_ref, v_ref, vt_ref, f_ref, lb_ref, o_ref, st_ref = refs

    @pl.when(pl.program_id(2) == 0)
    def _():
        st_ref[...] = jnp.zeros(st_ref.shape, F32)

    lp = lb_ref[0]
    e = jnp.exp(lp - jnp.max(lp, axis=0, keepdims=True))
    lb = jnp.sum(e[0:level + 1], axis=0, keepdims=True) / jnp.sum(e, axis=0, keepdims=True)

    tri, masks, diag = _hgrn_masks(rev)
    c = HGRN_CHUNK
    st = st_ref[...]
    order = range(nchunk - 1, -1, -1) if rev else range(nchunk)
    for ci in order:
        rows = slice(ci * c, (ci + 1) * c)
        f = lb + (1.0 - lb) * _sigmoid(f_ref[0, rows, :])
        out, st = _hgrn_chunk(q_ref[0, rows, :].astype(F32), 1.0 - f, jnp.log(f), v_ref[0, rows, :],
                              vt_ref[0, :, rows], st, tri, masks, diag, rev)
        if rev:
            o = out + of_ref[0, rows, :]
            o_ref[0, rows, :] = (_rms(o, gn_ref[...]) * g_ref[0, rows, :]).astype(BF16)
        else:
            o_ref[0, rows, :] = out
    st_ref[...] = st


def _hgrn_dir(hq, hi, hit, hf, lb_param, nh, rev, level, extra=()):
    b, t, _ = hq.shape
    step = min(HGRN_STEP, t)
    nstep = t // step
    d = HEAD_DIM
    tpos = (lambda n: nstep - 1 - n) if rev else (lambda n: n)
    row = lambda bi, h, n: (bi, tpos(n), h)
    in_specs = [pl.BlockSpec((1, step, d), row),
                pl.BlockSpec((1, step, d), row),
                pl.BlockSpec((1, d, step), lambda bi, h, n: (bi, h, tpos(n))),
                pl.BlockSpec((1, step, d), lambda bi, h, n: (bi, tpos(n), (nh if rev else 0) + h)),
                pl.BlockSpec((1, lb_param.shape[1], d), lambda bi, h, n: (1 if rev else 0, 0, h))]
    if rev:
        in_specs += [pl.BlockSpec((1, step, d), row), pl.BlockSpec((1, step, d), row),
                     pl.BlockSpec((1, d), lambda bi, h, n: (0, 0))]
    return pl.pallas_call(
        functools.partial(_hgrn_kernel, rev=rev, nchunk=step // HGRN_CHUNK, level=level),
        grid=(b, nh, nstep),
        in_specs=in_specs,
        out_specs=pl.BlockSpec((1, step, d), row),
        out_shape=jax.ShapeDtypeStruct((b, t, nh * d), BF16 if rev else F32),
        scratch_shapes=[pltpu.VMEM((d, d), F32)],
        compiler_params=_cparams(("parallel", "parallel", "arbitrary")),
        name="hgrn_bwd" if rev else "hgrn_fwd",
    )(hq, hi, hit, hf, lb_param, *extra)


def _out_proj_kernel(a_ref, o_ref, x_ref, w_ref, gp_ref, gf_ref, h_ref, u_ref, *, wd):
    m = _dot(a_ref[...], w_ref[0:wd, :]) + _dot(o_ref[...], w_ref[wd:, :])
    h = x_ref[...] + _rms(m, gp_ref[...])
    h_ref[...] = h
    u_ref[...] = _rms(h, gf_ref[...]).astype(BF16)


def _out_proj(a, o, x, w, g_post, g_ffn):
    n, d = x.shape
    wd = a.shape[1]
    tm = TOKEN_TILE
    row = lambda i: (i, 0)
    fixed = lambda i: (0, 0)
    return pl.pallas_call(
        functools.partial(_out_proj_kernel, wd=wd),
        grid=(n // tm,),
        in_specs=[pl.BlockSpec((tm, wd), row), pl.BlockSpec((tm, o.shape[1]), row), pl.BlockSpec((tm, d), row),
                  pl.BlockSpec(w.shape, fixed), pl.BlockSpec((1, d), fixed), pl.BlockSpec((1, d), fixed)],
        out_specs=[pl.BlockSpec((tm, d), row), pl.BlockSpec((tm, d), row)],
        out_shape=[jax.ShapeDtypeStruct((n, d), F32), jax.ShapeDtypeStruct((n, d), BF16)],
        compiler_params=_cparams(("parallel",)),
        name="out_proj",
    )(a, o, x, w, g_post, g_ffn)


def _ffn_kernel(u_ref, up_ref, un_ref, wg_ref, wu_ref, cwg_ref, cwu_ref, cbg_ref, cbu_ref, wd_ref,
                h_ref, g_ref, o_ref, acc_ref, *, tiles_per_seq):
    i = pl.program_id(0)
    j = pl.program_id(1)
    nj = pl.num_programs(1)
    tm = u_ref.shape[0]

    @pl.when(j == 0)
    def _():
        acc_ref[...] = jnp.zeros(acc_ref.shape, F32)

    u = u_ref[...]
    halo = jnp.concatenate([up_ref[...], un_ref[...]], axis=0)
    ti = i % tiles_per_seq
    has_prev = jnp.where(ti > 0, 1.0, 0.0)
    has_next = jnp.where(ti < tiles_per_seq - 1, 1.0, 0.0)
    rowi = lax.broadcasted_iota(jnp.int32, (tm, wg_ref.shape[1]), 0)

    def branch(w_ref, cw_ref, cb_ref):
        w = w_ref[...]
        a = _dot(u, w)
        ah = _dot(halo, w)
        prev_row = ah[HALO_ROWS - 1:HALO_ROWS, :] * has_prev
        next_row = ah[HALO_ROWS:HALO_ROWS + 1, :] * has_next
        a_prev = jnp.where(rowi == 0, prev_row, pltpu.roll(a, 1, 0))
        a_next = jnp.where(rowi == tm - 1, next_row, pltpu.roll(a, tm - 1, 0))
        cw = cw_ref[...]
        return cw[0:1, :] * a_prev + cw[1:2, :] * a + cw[2:3, :] * a_next + cb_ref[...]

    gate = branch(wg_ref, cwg_ref, cbg_ref)
    up = branch(wu_ref, cwu_ref, cbu_ref)
    c0 = math.sqrt(2.0 / math.pi)
    gelu = 0.5 * gate * (1.0 + jnp.tanh(c0 * (gate + 0.044715 * (gate * gate * gate))))
    acc_ref[...] += _dot((gelu * up).astype(BF16), wd_ref[...])

    @pl.when(j == nj - 1)
    def _():
        o_ref[...] = h_ref[...] + _rms(acc_ref[...], g_ref[...])


def _conv_ffn(u, h, w_up, conv_w, conv_b, w_down, g_post, seq):
    n, d = h.shape
    dff = w_down.shape[0]
    tm = min(FFN_TOKEN_TILE, seq)
    tf = FFN_COL_TILE
    nf = dff // tf
    hb = tm // HALO_ROWS
    nhalo = n // HALO_ROWS
    row = lambda i, j: (i, 0)
    return pl.pallas_call(
        functools.partial(_ffn_kernel, tiles_per_seq=seq // tm),
        grid=(n // tm, nf),
        in_specs=[pl.BlockSpec((tm, d), row),
                  pl.BlockSpec((HALO_ROWS, d), lambda i, j: (jnp.maximum(i * hb - 1, 0), 0)),
                  pl.BlockSpec((HALO_ROWS, d), lambda i, j: (jnp.minimum((i + 1) * hb, nhalo - 1), 0)),
                  pl.BlockSpec((d, tf), lambda i, j: (0, j)),
                  pl.BlockSpec((d, tf), lambda i, j: (0, nf + j)),
                  pl.BlockSpec((CONV_WIDTH, tf), lambda i, j: (0, j)),
                  pl.BlockSpec((CONV_WIDTH, tf), lambda i, j: (0, nf + j)),
                  pl.BlockSpec((1, tf), lambda i, j: (0, j)),
                  pl.BlockSpec((1, tf), lambda i, j: (0, nf + j)),
                  pl.BlockSpec((tf, d), lambda i, j: (j, 0)),
                  pl.BlockSpec((tm, d), row),
                  pl.BlockSpec((1, d), lambda i, j: (0, 0))],
        out_specs=pl.BlockSpec((tm, d), row),
        out_shape=jax.ShapeDtypeStruct((n, d), F32),
        scratch_shapes=[pltpu.VMEM((tm, d), F32)],
        compiler_params=_cparams(("parallel", "arbitrary")),
        name="conv_ffn",
    )(u, u, u, w_up, w_up, conv_w, conv_w, conv_b, conv_b, w_down, h, g_post)


def _ple_kernel(h_ref, p_ref, wp_ref, wg_ref, g_ref, o_ref):
    h = h_ref[...]
    e = _rms(_dot(p_ref[...].astype(BF16), wp_ref[...]), g_ref[...])
    gate = _sigmoid(_dot(h.astype(BF16), wg_ref[...]))
    o_ref[...] = h + e * gate


def _ple(h, p, w_ple, w_gate, g):
    n, d = h.shape
    tm = TOKEN_TILE
    row = lambda i: (i, 0)
    fixed = lambda i: (0, 0)
    return pl.pallas_call(
        _ple_kernel,
        grid=(n // tm,),
        in_specs=[pl.BlockSpec((tm, d), row), pl.BlockSpec((tm, p.shape[1]), row),
                  pl.BlockSpec(w_ple.shape, fixed), pl.BlockSpec(w_gate.shape, fixed), pl.BlockSpec((1, d), fixed)],
        out_specs=pl.BlockSpec((tm, d), row),
        out_shape=jax.ShapeDtypeStruct((n, d), F32),
        compiler_params=_cparams(("parallel",)),
        name="ple",
    )(h, p, w_ple, w_gate, g)


def kernel(x, p, rel_bias, g_pre_mix, w_in, lambda_q1, lambda_k1, lambda_q2, lambda_k2, g_diff, lb_param, g_hgrn, w_out, g_post_mix, g_pre_ffn, w_up, conv_w, conv_b, w_down, g_post_ffn, w_ple, g_ple, w_ple_gate):
    b, t, d = x.shape
    depth = w_in.shape[0]
    wd = d // 2
    wh = d - wd
    nh_diff = wd // HEAD_DIM
    nh_hgrn = wh // HEAD_DIM
    assert w_in.shape[2] == 3 * wd + 5 * wh
    assert t % ATT_BLOCK == 0 and t % HGRN_CHUNK == 0 and (b * t) % TOKEN_TILE == 0

    bias = _rel_bias_blocks(rel_bias.astype(F32), ATT_BLOCK)
    h = x
    for l in range(depth):
        vec = lambda a: a[l].reshape(1, -1).astype(F32)
        hid = h
        qt, k, vt, hq, hi, hit, hf, hg = _in_proj(hid, vec(g_pre_mix), w_in[l].astype(BF16), wd, wh)
        lam_init = 0.8 - 0.6 * math.exp(-0.3 * l)
        a = _diff_attn(qt, k, vt, bias, vec(lambda_q1), vec(lambda_k1), vec(lambda_q2), vec(lambda_k2),
                       vec(g_diff), nh_diff, lam_init)
        lbp = lb_param.astype(F32)
        o_fwd = _hgrn_dir(hq, hi, hit, hf, lbp, nh_hgrn, False, l)
        o = _hgrn_dir(hq, hi, hit, hf, lbp, nh_hgrn, True, l, extra=(o_fwd, hg, vec(g_hgrn)))
        n = b * t
        h1, u2 = _out_proj(a.reshape(n, wd), o.reshape(n, wh), hid.reshape(n, d), w_out[l].astype(BF16),
                           vec(g_post_mix), vec(g_pre_ffn))
        h2 = _conv_ffn(u2, h1, w_up[l].astype(BF16), conv_w[l].astype(F32), conv_b[l].reshape(1, -1).astype(F32),
                       w_down[l].astype(BF16), vec(g_post_ffn), t)
        h3 = _ple(h2, p[l].reshape(n, -1), w_ple[l].astype(BF16), w_ple_gate[l].astype(BF16), vec(g_ple))
        h = h3.reshape(b, t, d)
    return h
```

```python
import functools
import math

import jax
import jax.numpy as jnp
from jax import lax
from jax.experimental import pallas as pl
from jax.experimental.pallas import tpu as pltpu

F32 = jnp.float32
BF16 = jnp.bfloat16

EPS = 1e-6
SUBLANES = 8
LANES = 128
DIFF_QK_DIM = 64
HEAD_DIM = 128
REL_BUCKETS = 32
REL_MAX_DIST = 128
CONV_WIDTH = 3

ATT_BLOCK = 512
ATT_STRIP = 128
ONES_ROWS = 16
LOG2E = math.log2(math.e)
HGRN_CHUNK = 128
HGRN_BASE = 16
HGRN_STEP = 1024
TOKEN_TILE = 512
FFN_TOKEN_TILE = 1024
FFN_COL_TILE = 1024
FFN_SUB_TILE = 256
HALO_ROWS = 16
VMEM_LIMIT = 56 * 1024 * 1024


def _cparams(sem, flags=None):
    return pltpu.CompilerParams(dimension_semantics=sem, vmem_limit_bytes=VMEM_LIMIT, flags=flags)


def _rms(x, g):
    return x * lax.rsqrt(jnp.mean(x * x, axis=-1, keepdims=True) + EPS) * g


def _sigmoid(x):
    return 1.0 / (1.0 + jnp.exp(-x))


def _dot(a, b):
    return jnp.dot(a, b, preferred_element_type=F32)


def _dot_nt(a, b):
    return lax.dot_general(a, b, (((1,), (1,)), ((), ())), preferred_element_type=F32)


def _log_bucket_thresholds():
    half = REL_BUCKETS // 2
    max_exact = half // 2
    nlog = half - max_exact
    thr = []
    for k in range(1, nlog):
        n = max_exact
        while (n ** nlog) * (max_exact ** k) < (REL_MAX_DIST ** k) * (max_exact ** nlog):
            n += 1
        thr.append(n)
    return max_exact, half, thr


def _bias_kernel(rb_ref, o_ref, *, blk):
    h = pl.program_id(0)
    d = pl.program_id(1) - 2
    max_exact, half, thr = _log_bucket_thresholds()
    krow = lax.broadcasted_iota(jnp.int32, (blk, blk), 0)
    qcol = lax.broadcasted_iota(jnp.int32, (blk, blk), 1)
    rel = d * blk + krow - qcol
    n = jnp.abs(rel)
    large = jnp.full((blk, blk), max_exact, jnp.int32)
    for t in thr:
        large = large + (n >= t).astype(jnp.int32)
    bucket = jnp.where(rel > 0, half, 0) + jnp.where(n < max_exact, n, large)
    bias = jnp.zeros((blk, blk), F32)
    for b in range(REL_BUCKETS):
        bias = jnp.where(bucket == b, rb_ref[b, h], bias)
    o_ref[0, 0] = bias * LOG2E


def _rel_bias_blocks(rel_bias, blk):
    assert blk >= REL_MAX_DIST
    nh = rel_bias.shape[1]
    return pl.pallas_call(
        functools.partial(_bias_kernel, blk=blk),
        grid=(nh, 5),
        in_specs=[pl.BlockSpec(memory_space=pltpu.SMEM)],
        out_specs=pl.BlockSpec((1, 1, blk, blk), lambda h, d: (h, d, 0, 0)),
        out_shape=jax.ShapeDtypeStruct((nh, 5, blk, blk), F32),
        compiler_params=_cparams(("parallel", "parallel")),
        name="rel_bias_blocks",
    )(rel_bias)


def _in_proj_kernel(x_ref, g_ref, w_ref, qt_ref, k_ref, vt_ref, hq_ref, hi_ref, hit_ref, hf_ref, hg_ref,
                    *, wd, wh):
    u = _rms(x_ref[0], g_ref[...]).astype(BF16)

    def proj(a, b):
        return _dot(u, w_ref[:, a:b])

    qt_ref[0] = (proj(0, wd) * (DIFF_QK_DIM ** -0.5 * LOG2E)).T.astype(BF16)
    kp = proj(wd, 2 * wd).astype(BF16)
    for h in range(wd // HEAD_DIM):
        k_ref[0, h] = kp[:, h * HEAD_DIM:(h + 1) * HEAD_DIM]
    vt_ref[0] = proj(2 * wd, 3 * wd).T.astype(BF16)
    o = 3 * wd
    qh = proj(o, o + wh)
    hq_ref[0] = (qh * _sigmoid(qh)).astype(BF16)
    ih = proj(o + wh, o + 2 * wh)
    hi_ref[0] = ih.astype(BF16)
    hit_ref[0] = ih.T.astype(BF16)
    hf_ref[0] = proj(o + 2 * wh, o + 4 * wh)
    gh = proj(o + 4 * wh, o + 5 * wh)
    hg_ref[0] = gh * _sigmoid(gh)


def _in_proj(x, g, w, wd, wh):
    b, t, d = x.shape
    tm = TOKEN_TILE
    ncol = w.shape[1]
    row = lambda bi, ti: (bi, ti, 0)
    col = lambda bi, ti: (bi, 0, ti)
    return pl.pallas_call(
        functools.partial(_in_proj_kernel, wd=wd, wh=wh),
        grid=(b, t // tm),
        in_specs=[pl.BlockSpec((1, tm, d), row),
                  pl.BlockSpec((1, d), lambda bi, ti: (0, 0)),
                  pl.BlockSpec((d, ncol), lambda bi, ti: (0, 0))],
        out_specs=[pl.BlockSpec((1, wd, tm), col),
                   pl.BlockSpec((1, wd // HEAD_DIM, tm, HEAD_DIM), lambda bi, ti: (bi, 0, ti, 0)),
                   pl.BlockSpec((1, wd, tm), col),
                   pl.BlockSpec((1, tm, wh), row),
                   pl.BlockSpec((1, tm, wh), row),
                   pl.BlockSpec((1, wh, tm), col),
                   pl.BlockSpec((1, tm, 2 * wh), row),
                   pl.BlockSpec((1, tm, wh), row)],
        out_shape=[jax.ShapeDtypeStruct((b, wd, t), BF16),
                   jax.ShapeDtypeStruct((b, wd // HEAD_DIM, t, HEAD_DIM), BF16),
                   jax.ShapeDtypeStruct((b, wd, t), BF16),
                   jax.ShapeDtypeStruct((b, t, wh), BF16),
                   jax.ShapeDtypeStruct((b, t, wh), BF16),
                   jax.ShapeDtypeStruct((b, wh, t), BF16),
                   jax.ShapeDtypeStruct((b, t, 2 * wh), F32),
                   jax.ShapeDtypeStruct((b, t, wh), F32)],
        compiler_params=_cparams(("parallel", "parallel")),
        name="in_proj",
    )(x, g, w)


def _attn_kernel(qt_ref, k_ref, vt_ref, bias_ref, lq1_ref, lk1_ref, lq2_ref, lk2_ref, gd_ref,
                 o_ref, m_ref, a_ref, acc_ref, s_ref, p_ref, *, blk, strip, seq, lam_init):
    qi = pl.program_id(2)
    nblk = seq // blk
    qt = qt_ref[0]
    sub = lax.broadcasted_iota(jnp.int32, qt.shape, 0)
    zero = jnp.zeros_like(qt)
    qqt = jnp.concatenate([jnp.where(sub < DIFF_QK_DIM, qt, zero),
                           jnp.where(sub >= DIFF_QK_DIM, qt, zero)], axis=1)
    m_ref[...] = jnp.full(m_ref.shape, -jnp.inf, F32)
    acc_ref[...] = jnp.zeros(acc_ref.shape, F32)
    ones = jnp.ones((ONES_ROWS, blk), BF16)

    nstrip = blk // strip

    def score_strip(kb, slot, r, mx8):
        didx = jnp.clip(kb - qi, -2, 2) + 2
        rows = slice(r * strip, (r + 1) * strip)
        b = bias_ref[0, didx, rows, :]
        kc = k_ref[0, 0, pl.ds(pl.multiple_of(kb * blk, blk) + r * strip, strip), :]
        s = _dot(kc, qqt) + jnp.concatenate([b, b], axis=1)
        s_ref[slot, rows, :] = s
        for i in range(strip // 8):
            mx8 = jnp.maximum(mx8, s[i * 8:(i + 1) * 8])
        return mx8

    def accumulate(kb, slot):
        k0 = pl.multiple_of(kb * blk, blk)
        vta = jnp.concatenate([vt_ref[0, :, pl.ds(k0, blk)], ones], axis=0)
        acc_ref[...] = a_ref[slot] * acc_ref[...] + _dot(vta, p_ref[slot])

    def stage(kb, slot, mx8, do_scores=True, do_values=True):
        m_prev = m_ref[...]
        m_new = jnp.maximum(m_prev, jnp.max(mx8, axis=0, keepdims=True))
        m_ref[...] = m_new
        a_ref[slot] = jnp.exp2(m_prev - m_new)
        nxt = jnp.full((8, 2 * blk), -jnp.inf, F32)
        for r in range(nstrip):
            if do_scores:
                nxt = score_strip(kb + 1, 1 - slot, r, nxt)
            rows = slice(r * strip, (r + 1) * strip)
            p_ref[slot, rows, :] = jnp.exp2(s_ref[slot, rows, :] - m_new).astype(BF16)
        if do_values:
            accumulate(kb - 1, 1 - slot)
        return nxt

    mx8 = jnp.full((8, 2 * blk), -jnp.inf, F32)
    for r in range(nstrip):
        mx8 = score_strip(0, 0, r, mx8)
    for kb in range(nblk):
        mx8 = stage(kb, kb % 2, mx8, do_scores=kb + 1 < nblk, do_values=kb > 0)
    accumulate(nblk - 1, (nblk - 1) % 2)

    lam = (jnp.exp(jnp.sum(lq1_ref[...] * lk1_ref[...])) - jnp.exp(jnp.sum(lq2_ref[...] * lk2_ref[...]))
           + lam_init)
    acc = acc_ref[...]
    ot = acc[:HEAD_DIM] / acc[HEAD_DIM:HEAD_DIM + 1]
    o = (ot[:, :blk] - lam * ot[:, blk:]).T
    o_ref[0] = (_rms(o, gd_ref[...]) * (1.0 - lam_init)).astype(BF16)


def _diff_attn(qt, k, vt, bias, lq1, lk1, lq2, lk2, g_diff, nh, lam_init):
    b, _, t, _ = k.shape
    blk = ATT_BLOCK
    assert (t // blk) % 2 == 0
    small = lambda bi, h, qi: (0, 0)
    return pl.pallas_call(
        functools.partial(_attn_kernel, blk=blk, strip=ATT_STRIP, seq=t, lam_init=lam_init),
        grid=(b, nh, t // blk),
        in_specs=[pl.BlockSpec((1, HEAD_DIM, blk), lambda bi, h, qi: (bi, h, qi)),
                  pl.BlockSpec((1, 1, t, HEAD_DIM), lambda bi, h, qi: (bi, h, 0, 0)),
                  pl.BlockSpec((1, HEAD_DIM, t), lambda bi, h, qi: (bi, h, 0)),
                  pl.BlockSpec((1, 5, blk, blk), lambda bi, h, qi: (h, 0, 0, 0)),
                  pl.BlockSpec((1, DIFF_QK_DIM), small), pl.BlockSpec((1, DIFF_QK_DIM), small),
                  pl.BlockSpec((1, DIFF_QK_DIM), small), pl.BlockSpec((1, DIFF_QK_DIM), small),
                  pl.BlockSpec((1, HEAD_DIM), small)],
        out_specs=pl.BlockSpec((1, blk, HEAD_DIM), lambda bi, h, qi: (bi, qi, h)),
        out_shape=jax.ShapeDtypeStruct((b, t, nh * HEAD_DIM), BF16),
        scratch_shapes=[pltpu.VMEM((1, 2 * blk), F32),
                        pltpu.VMEM((2, 1, 2 * blk), F32),
                        pltpu.VMEM((HEAD_DIM + ONES_ROWS, 2 * blk), F32),
                        pltpu.VMEM((2, blk, 2 * blk), F32),
                        pltpu.VMEM((2, blk, 2 * blk), BF16)],
        compiler_params=_cparams(("parallel", "parallel", "arbitrary")),
        name="diff_attn",
    )(qt, k, vt, bias, lq1, lk1, lq2, lk2, g_diff)


def _hgrn_masks(rev):
    c = HGRN_CHUNK
    row = lax.broadcasted_iota(jnp.int32, (c, c), 0)
    col = lax.broadcasted_iota(jnp.int32, (c, c), 1)
    tri = (col >= row) if rev else (col <= row)
    masks = []
    b = HGRN_BASE
    while b < c:
        masks.append((b, ((row ^ col) >> int(math.log2(b))) == 1))
        b *= 2
    sh = int(math.log2(HGRN_BASE))
    diag = ((row >> sh) == (col >> sh)) & tri
    return tri.astype(BF16), masks, diag


def _row_refs(cum, idxs, rows_each):
    return jnp.concatenate(
        [jnp.broadcast_to(cum[i:i + 1, :], (rows_each, cum.shape[1])) for i in idxs], axis=0)


def _hgrn_chunks(q, kk, logf, vs, vts, st, tri, masks, diag, rev):
    c = HGRN_CHUNK
    n = len(vs)
    d = q.shape[1] // n
    part = lambda x, i: x[:, i * d:(i + 1) * d]
    hi = logf.astype(BF16)
    r1 = logf - hi.astype(F32)
    mid = r1.astype(BF16)
    lo = (r1 - mid.astype(F32)).astype(BF16)
    cum = _dot(tri, hi) + _dot(tri, mid) + _dot(tri, lo)
    last = cum[0:1, :] if rev else cum[c - 1:c, :]
    rowi = lax.broadcasted_iota(jnp.int32, q.shape, 0)

    a = [jnp.zeros((c, c), F32)] * n
    for b, siblings in masks:
        ngroups = c // (2 * b)
        ref = _row_refs(cum, [g * 2 * b + (b if rev else b - 1) for g in range(ngroups)], 2 * b)
        first_half = (rowi & (2 * b - 1)) < b
        is_q = first_half if rev else jnp.logical_not(first_half)
        qt = jnp.where(is_q, q * jnp.exp(jnp.minimum(cum - ref, 0.0)), 0.0).astype(BF16)
        kt = jnp.where(is_q, 0.0, kk * jnp.exp(jnp.minimum(ref - cum, 0.0))).astype(BF16)
        a = [jnp.where(siblings, _dot_nt(part(qt, i), part(kt, i)), a[i]) for i in range(n)]
    nb = c // HGRN_BASE
    ref = _row_refs(cum, [g * HGRN_BASE + HGRN_BASE // 2 for g in range(nb)], HGRN_BASE)
    qd = (q * jnp.exp(cum - ref)).astype(BF16)
    kd = (kk * jnp.exp(ref - cum)).astype(BF16)
    a = [jnp.where(diag, _dot_nt(part(qd, i), part(kd, i)), a[i]) for i in range(n)]

    qe = (q * jnp.exp(cum)).astype(BF16)
    kl = (kk * jnp.exp(last - cum)).astype(BF16)
    decay = jnp.exp(last)
    intra = [_dot(a[i].astype(BF16), vs[i]) for i in range(n)]
    update = [_dot(vts[i], part(kl, i)) for i in range(n)]
    outs = [None] * n
    for i in (range(n - 1, -1, -1) if rev else range(n)):
        outs[i] = intra[i] + _dot_nt(part(qe, i), st.astype(BF16))
        st = st * part(decay, i) + update[i]
    return outs, st


def _hgrn_kernel(*refs, rev, nchunk, level):
    if rev:
        q_ref, v_ref, vt_ref, f_ref, lb_ref, of_ref, g_ref, gn_ref, o_ref, st_ref = refs
    else:
        q_ref, v_ref, vt_ref, f_ref, lb_ref, o_ref, st_ref = refs

    @pl.when(pl.program_id(2) == 0)
    def _():
        st_ref[...] = jnp.zeros(st_ref.shape, F32)

    lp = lb_ref[0]
    e = jnp.exp(lp - jnp.max(lp, axis=0, keepdims=True))
    lb = jnp.sum(e[0:level + 1], axis=0, keepdims=True) / jnp.sum(e, axis=0, keepdims=True)

    tri, masks, diag = _hgrn_masks(rev)
    c = HGRN_CHUNK
    chunks = [slice(ci * c, (ci + 1) * c) for ci in range(nchunk)]
    wide = lambda ref: jnp.concatenate([ref[0, rows, :] for rows in chunks], axis=1)
    lbw = jnp.concatenate([lb] * nchunk, axis=1)
    f = lbw + (1.0 - lbw) * _sigmoid(wide(f_ref))
    outs, st = _hgrn_chunks(wide(q_ref).astype(F32), 1.0 - f, jnp.log(f),
                            [v_ref[0, rows, :] for rows in chunks], [vt_ref[0, :, rows] for rows in chunks],
                            st_ref[...], tri, masks, diag, rev)
    st_ref[...] = st
    for rows, out in zip(chunks, outs):
        if rev:
            o = out + of_ref[0, rows, :]
            o_ref[0, rows, :] = (_rms(o, gn_ref[...]) * g_ref[0, rows, :]).astype(BF16)
        else:
            o_ref[0, rows, :] = out


def _hgrn_dir(hq, hi, hit, hf, lb_param, nh, rev, level, extra=()):
    b, t, _ = hq.shape
    step = min(HGRN_STEP, t)
    nstep = t // step
    d = HEAD_DIM
    tpos = (lambda n: nstep - 1 - n) if rev else (lambda n: n)
    row = lambda bi, h, n: (bi, tpos(n), h)
    in_specs = [pl.BlockSpec((1, step, d), row),
                pl.BlockSpec((1, step, d), row),
                pl.BlockSpec((1, d, step), lambda bi, h, n: (bi, h, tpos(n))),
                pl.BlockSpec((1, step, d), lambda bi, h, n: (bi, tpos(n), (nh if rev else 0) + h)),
                pl.BlockSpec((1, lb_param.shape[1], d), lambda bi, h, n: (1 if rev else 0, 0, h))]
    if rev:
        in_specs += [pl.BlockSpec((1, step, d), row), pl.BlockSpec((1, step, d), row),
                     pl.BlockSpec((1, d), lambda bi, h, n: (0, 0))]
    return pl.pallas_call(
        functools.partial(_hgrn_kernel, rev=rev, nchunk=step // HGRN_CHUNK, level=level),
        grid=(b, nh, nstep),
        in_specs=in_specs,
        out_specs=pl.BlockSpec((1, step, d), row),
        out_shape=jax.ShapeDtypeStruct((b, t, nh * d), BF16 if rev else F32),
        scratch_shapes=[pltpu.VMEM((d, d), F32)],
        compiler_params=_cparams(("parallel", "parallel", "arbitrary")),
        name="hgrn_bwd" if rev else "hgrn_fwd",
    )(hq, hi, hit, hf, lb_param, *extra)


def _out_proj_kernel(a_ref, o_ref, x_ref, w_ref, gp_ref, gf_ref, h_ref, u_ref, *, wd):
    m = _dot(a_ref[...], w_ref[0:wd, :]) + _dot(o_ref[...], w_ref[wd:, :])
    h = x_ref[...] + _rms(m, gp_ref[...])
    h_ref[...] = h
    u_ref[...] = _rms(h, gf_ref[...]).astype(BF16)


def _out_proj(a, o, x, w, g_post, g_ffn):
    n, d = x.shape
    wd = a.shape[1]
    tm = TOKEN_TILE
    row = lambda i: (i, 0)
    fixed = lambda i: (0, 0)
    return pl.pallas_call(
        functools.partial(_out_proj_kernel, wd=wd),
        grid=(n // tm,),
        in_specs=[pl.BlockSpec((tm, wd), row), pl.BlockSpec((tm, o.shape[1]), row), pl.BlockSpec((tm, d), row),
                  pl.BlockSpec(w.shape, fixed), pl.BlockSpec((1, d), fixed), pl.BlockSpec((1, d), fixed)],
        out_specs=[pl.BlockSpec((tm, d), row), pl.BlockSpec((tm, d), row)],
        out_shape=[jax.ShapeDtypeStruct((n, d), F32), jax.ShapeDtypeStruct((n, d), BF16)],
        compiler_params=_cparams(("parallel",)),
        name="out_proj",
    )(a, o, x, w, g_post, g_ffn)


def _ffn_kernel(u_ref, up_ref, un_ref, wg_ref, wu_ref, cwg_ref, cwu_ref, cbg_ref, cbu_ref, wd_ref,
                h_ref, g_ref, o_ref, acc_ref, sh_ref, *, tiles_per_seq, sub):
    i = pl.program_id(0)
    j = pl.program_id(1)
    nj = pl.num_programs(1)
    tm = u_ref.shape[0]

    @pl.when(j == 0)
    def _():
        acc_ref[...] = jnp.zeros(acc_ref.shape, F32)

    u = u_ref[...]
    halo = jnp.concatenate([up_ref[...], un_ref[...]], axis=0)
    ti = i % tiles_per_seq
    has_prev = jnp.where(ti > 0, 1.0, 0.0)
    has_next = jnp.where(ti < tiles_per_seq - 1, 1.0, 0.0)
    c0 = math.sqrt(2.0 / math.pi)
    c1 = c0 * 0.044715
    lane_tiles = sub // LANES
    pad = SUBLANES

    def project(c):
        cols = slice(c * sub, (c + 1) * sub)
        return [(_dot(u, w_ref[:, cols]), _dot(halo, w_ref[:, cols])) for w_ref in (wg_ref, wu_ref)]

    def conv(c, branch, a, ah, cw_ref, cb_ref):
        outs = []
        for lt in range(lane_tiles):
            slab = ((c % 2) * 2 + branch) * lane_tiles + lt
            lanes = slice(lt * LANES, (lt + 1) * LANES)
            cols = slice(c * sub + lt * LANES, c * sub + (lt + 1) * LANES)
            al = a[:, lanes]
            sh_ref[slab, pad - 1:pad, :] = ah[HALO_ROWS - 1:HALO_ROWS, lanes] * has_prev
            sh_ref[slab, pad:pad + tm, :] = al
            sh_ref[slab, pad + tm:pad + tm + 1, :] = ah[HALO_ROWS:HALO_ROWS + 1, lanes] * has_next
            cw = cw_ref[:, cols]
            outs.append(cw[0:1, :] * sh_ref[slab, pad - 1:pad - 1 + tm, :] + cw[1:2, :] * al
                        + cw[2:3, :] * sh_ref[slab, pad + 1:pad + 1 + tm, :] + cb_ref[:, cols])
        return jnp.concatenate(outs, axis=1)

    def activate(c, projected):
        gate, up = [conv(c, br, a, ah, cw_ref, cb_ref) for br, ((a, ah), cw_ref, cb_ref)
                    in enumerate(zip(projected, (cwg_ref, cwu_ref), (cbg_ref, cbu_ref)))]
        return (gate * up * (1.0 + jnp.tanh(gate * (c0 + c1 * (gate * gate))))).astype(BF16)

    nsub = wg_ref.shape[1] // sub
    projected = [project(c) for c in range(min(2, nsub))]
    for c in range(nsub):
        if c + 2 < nsub:
            projected.append(project(c + 2))
        acc_ref[...] += _dot(activate(c, projected[c]), wd_ref[c * sub:(c + 1) * sub, :])

    @pl.when(j == nj - 1)
    def _():
        o_ref[...] = h_ref[...] + _rms(acc_ref[...], g_ref[...])


def _conv_ffn(u, h, w_up, conv_w, conv_b, w_down, g_post, seq):
    n, d = h.shape
    dff = w_down.shape[0]
    tm = min(FFN_TOKEN_TILE, seq)
    tf = FFN_COL_TILE
    nf = dff // tf
    hb = tm // HALO_ROWS
    nhalo = n // HALO_ROWS
    row = lambda i, j: (i, 0)
    return pl.pallas_call(
        functools.partial(_ffn_kernel, tiles_per_seq=seq // tm, sub=FFN_SUB_TILE),
        grid=(n // tm, nf),
        in_specs=[pl.BlockSpec((tm, d), row),
                  pl.BlockSpec((HALO_ROWS, d), lambda i, j: (jnp.maximum(i * hb - 1, 0), 0)),
                  pl.BlockSpec((HALO_ROWS, d), lambda i, j: (jnp.minimum((i + 1) * hb, nhalo - 1), 0)),
                  pl.BlockSpec((d, tf), lambda i, j: (0, j)),
                  pl.BlockSpec((d, tf), lambda i, j: (0, nf + j)),
                  pl.BlockSpec((CONV_WIDTH, tf), lambda i, j: (0, j)),
                  pl.BlockSpec((CONV_WIDTH, tf), lambda i, j: (0, nf + j)),
                  pl.BlockSpec((1, tf), lambda i, j: (0, j)),
                  pl.BlockSpec((1, tf), lambda i, j: (0, nf + j)),
                  pl.BlockSpec((tf, d), lambda i, j: (j, 0)),
                  pl.BlockSpec((tm, d), row),
                  pl.BlockSpec((1, d), lambda i, j: (0, 0))],
        out_specs=pl.BlockSpec((tm, d), row),
        out_shape=jax.ShapeDtypeStruct((n, d), F32),
        scratch_shapes=[pltpu.VMEM((tm, d), F32),
                        pltpu.VMEM((4 * FFN_SUB_TILE // LANES, tm + 2 * SUBLANES, LANES), F32)],
        compiler_params=_cparams(("parallel", "arbitrary")),
        name="conv_ffn",
    )(u, u, u, w_up, w_up, conv_w, conv_w, conv_b, conv_b, w_down, h, g_post)


def _ple_kernel(h_ref, p_ref, wp_ref, wg_ref, g_ref, o_ref):
    h = h_ref[...]
    e = _rms(_dot(p_ref[...].astype(BF16), wp_ref[...]), g_ref[...])
    gate = _sigmoid(_dot(h.astype(BF16), wg_ref[...]))
    o_ref[...] = h + e * gate


def _ple(h, p, w_ple, w_gate, g):
    n, d = h.shape
    tm = TOKEN_TILE
    row = lambda i: (i, 0)
    fixed = lambda i: (0, 0)
    return pl.pallas_call(
        _ple_kernel,
        grid=(n // tm,),
        in_specs=[pl.BlockSpec((tm, d), row), pl.BlockSpec((tm, p.shape[1]), row),
                  pl.BlockSpec(w_ple.shape, fixed), pl.BlockSpec(w_gate.shape, fixed), pl.BlockSpec((1, d), fixed)],
        out_specs=pl.BlockSpec((tm, d), row),
        out_shape=jax.ShapeDtypeStruct((n, d), F32),
        compiler_params=_cparams(("parallel",)),
        name="ple",
    )(h, p, w_ple, w_gate, g)


def kernel(x, p, rel_bias, g_pre_mix, w_in, lambda_q1, lambda_k1, lambda_q2, lambda_k2, g_diff, lb_param, g_hgrn, w_out, g_post_mix, g_pre_ffn, w_up, conv_w, conv_b, w_down, g_post_ffn, w_ple, g_ple, w_ple_gate):
    b, t, d = x.shape
    depth = w_in.shape[0]
    wd = d // 2
    wh = d - wd
    nh_diff = wd // HEAD_DIM
    nh_hgrn = wh // HEAD_DIM
    assert w_in.shape[2] == 3 * wd + 5 * wh
    assert t % ATT_BLOCK == 0 and t % HGRN_CHUNK == 0 and (b * t) % TOKEN_TILE == 0

    bias = _rel_bias_blocks(rel_bias.astype(F32), ATT_BLOCK)
    h = x
    for l in range(depth):
        vec = lambda a: a[l].reshape(1, -1).astype(F32)
        hid = h
        qt, k, vt, hq, hi, hit, hf, hg = _in_proj(hid, vec(g_pre_mix), w_in[l].astype(BF16), wd, wh)
        lam_init = 0.8 - 0.6 * math.exp(-0.3 * l)
        a = _diff_attn(qt, k, vt, bias, vec(lambda_q1), vec(lambda_k1), vec(lambda_q2), vec(lambda_k2),
                       vec(g_diff), nh_diff, lam_init)
        lbp = lb_param.astype(F32)
        o_fwd = _hgrn_dir(hq, hi, hit, hf, lbp, nh_hgrn, False, l)
        o = _hgrn_dir(hq, hi, hit, hf, lbp, nh_hgrn, True, l, extra=(o_fwd, hg, vec(g_hgrn)))
        n = b * t
        h1, u2 = _out_proj(a.reshape(n, wd), o.reshape(n, wh), hid.reshape(n, d), w_out[l].astype(BF16),
                           vec(g_post_mix), vec(g_pre_ffn))
        h2 = _conv_ffn(u2, h1, w_up[l].astype(BF16), conv_w[l].astype(F32), conv_b[l].reshape(1, -1).astype(F32),
                       (0.5 * w_down[l]).astype(BF16), vec(g_post_ffn), t)
        h3 = _ple(h2, p[l].reshape(n, -1), w_ple[l].astype(BF16), w_ple_gate[l].astype(BF16), vec(g_ple))
        h = h3.reshape(b, t, d)
    return h
```

```python
import functools
import math

import jax
import jax.numpy as jnp
from jax import lax
from jax.experimental import pallas as pl
from jax.experimental.pallas import tpu as pltpu

F32 = jnp.float32
BF16 = jnp.bfloat16

EPS = 1e-6
SUBLANES = 8
LANES = 128
DIFF_QK_DIM = 64
HEAD_DIM = 128
REL_BUCKETS = 32
REL_MAX_DIST = 128
CONV_WIDTH = 3

ATT_BLOCK = 512
ATT_STRIP = 512
ATT_QBLOCKS = 2
NEAR_BLOCKS = 3
ONES_ROWS = 16
LOG2E = math.log2(math.e)
TOP16 = -65536
HGRN_CHUNK = 128
HGRN_BASE = 16
HGRN_STEP = 1024
TOKEN_TILE = 512
FFN_TOKEN_TILE = 1024
FFN_COL_TILE = 1024
FFN_SUB_TILE = 256
HALO_ROWS = 16
VMEM_LIMIT = 56 * 1024 * 1024


def _cparams(sem, flags=None):
    return pltpu.CompilerParams(dimension_semantics=sem, vmem_limit_bytes=VMEM_LIMIT, flags=flags)


def _rms(x, g):
    return x * lax.rsqrt(jnp.mean(x * x, axis=-1, keepdims=True) + EPS) * g


def _sigmoid(x):
    return 1.0 / (1.0 + jnp.exp(-x))


def _dot(a, b):
    return jnp.dot(a, b, preferred_element_type=F32)


def _dot_nt(a, b):
    return lax.dot_general(a, b, (((1,), (1,)), ((), ())), preferred_element_type=F32)


def _log_bucket_thresholds():
    half = REL_BUCKETS // 2
    max_exact = half // 2
    nlog = half - max_exact
    thr = []
    for k in range(1, nlog):
        n = max_exact
        while (n ** nlog) * (max_exact ** k) < (REL_MAX_DIST ** k) * (max_exact ** nlog):
            n += 1
        thr.append(n)
    return max_exact, half, thr


def _bias_kernel(rb_ref, o_ref, *, blk):
    h = pl.program_id(0)
    d = pl.program_id(1) - 2
    max_exact, half, thr = _log_bucket_thresholds()
    krow = lax.broadcasted_iota(jnp.int32, (blk, blk), 0)
    qcol = lax.broadcasted_iota(jnp.int32, (blk, blk), 1)
    rel = d * blk + krow - qcol
    n = jnp.abs(rel)
    large = jnp.full((blk, blk), max_exact, jnp.int32)
    for t in thr:
        large = large + (n >= t).astype(jnp.int32)
    bucket = jnp.where(rel > 0, half, 0) + jnp.where(n < max_exact, n, large)
    bias = jnp.zeros((blk, blk), F32)
    for b in range(REL_BUCKETS):
        bias = jnp.where(bucket == b, rb_ref[b, h], bias)
    o_ref[0, 0] = bias * LOG2E


def _rel_bias_blocks(rel_bias, blk):
    assert blk >= REL_MAX_DIST
    nh = rel_bias.shape[1]
    return pl.pallas_call(
        functools.partial(_bias_kernel, blk=blk),
        grid=(nh, 5),
        in_specs=[pl.BlockSpec(memory_space=pltpu.SMEM)],
        out_specs=pl.BlockSpec((1, 1, blk, blk), lambda h, d: (h, d, 0, 0)),
        out_shape=jax.ShapeDtypeStruct((nh, 5, blk, blk), F32),
        compiler_params=_cparams(("parallel", "parallel")),
        name="rel_bias_blocks",
    )(rel_bias)


def _in_proj_kernel(x_ref, g_ref, w_ref, qt_ref, k_ref, vt_ref, hq_ref, hi_ref, hit_ref, hf_ref, hg_ref,
                    *, wd, wh):
    u = _rms(x_ref[0], g_ref[...]).astype(BF16)

    def proj(a, b):
        return _dot(u, w_ref[:, a:b])

    qt_ref[0] = (proj(0, wd) * (DIFF_QK_DIM ** -0.5 * LOG2E)).T.astype(BF16)
    kp = proj(wd, 2 * wd).astype(BF16)
    for h in range(wd // HEAD_DIM):
        k_ref[0, h] = kp[:, h * HEAD_DIM:(h + 1) * HEAD_DIM]
    vt_ref[0] = proj(2 * wd, 3 * wd).T.astype(BF16)
    o = 3 * wd
    qh = proj(o, o + wh)
    hq_ref[0] = (qh * _sigmoid(qh)).astype(BF16)
    ih = proj(o + wh, o + 2 * wh)
    hi_ref[0] = ih.astype(BF16)
    hit_ref[0] = ih.T.astype(BF16)
    hf_ref[0] = proj(o + 2 * wh, o + 4 * wh)
    gh = proj(o + 4 * wh, o + 5 * wh)
    hg_ref[0] = gh * _sigmoid(gh)


def _in_proj(x, g, w, wd, wh):
    b, t, d = x.shape
    tm = TOKEN_TILE
    ncol = w.shape[1]
    row = lambda bi, ti: (bi, ti, 0)
    col = lambda bi, ti: (bi, 0, ti)
    return pl.pallas_call(
        functools.partial(_in_proj_kernel, wd=wd, wh=wh),
        grid=(b, t // tm),
        in_specs=[pl.BlockSpec((1, tm, d), row),
                  pl.BlockSpec((1, d), lambda bi, ti: (0, 0)),
                  pl.BlockSpec((d, ncol), lambda bi, ti: (0, 0))],
        out_specs=[pl.BlockSpec((1, wd, tm), col),
                   pl.BlockSpec((1, wd // HEAD_DIM, tm, HEAD_DIM), lambda bi, ti: (bi, 0, ti, 0)),
                   pl.BlockSpec((1, wd, tm), col),
                   pl.BlockSpec((1, tm, wh), row),
                   pl.BlockSpec((1, tm, wh), row),
                   pl.BlockSpec((1, wh, tm), col),
                   pl.BlockSpec((1, tm, 2 * wh), row),
                   pl.BlockSpec((1, tm, wh), row)],
        out_shape=[jax.ShapeDtypeStruct((b, wd, t), BF16),
                   jax.ShapeDtypeStruct((b, wd // HEAD_DIM, t, HEAD_DIM), BF16),
                   jax.ShapeDtypeStruct((b, wd, t), BF16),
                   jax.ShapeDtypeStruct((b, t, wh), BF16),
                   jax.ShapeDtypeStruct((b, t, wh), BF16),
                   jax.ShapeDtypeStruct((b, wh, t), BF16),
                   jax.ShapeDtypeStruct((b, t, 2 * wh), F32),
                   jax.ShapeDtypeStruct((b, t, wh), F32)],
        compiler_params=_cparams(("parallel", "parallel")),
        name="in_proj",
    )(x, g, w)


def _attn_kernel(rb_ref, qt_ref, k_ref, vt_ref, bias_ref, lq1_ref, lk1_ref, lq2_ref, lk2_ref, gd_ref,
                 o_ref, qq_ref, m_ref, st_ref, acc_ref, s_ref, p_ref, *, blk, strip, seq, nq, lam_init):
    h = pl.program_id(1)
    nblk = seq // blk
    nstrip = blk // strip
    ones = jnp.ones((ONES_ROWS, blk), BF16)
    half = REL_BUCKETS // 2
    far_left = rb_ref[half - 1, h] * LOG2E
    far_right = rb_ref[2 * half - 1, h] * LOG2E
    lam = (jnp.exp(jnp.sum(lq1_ref[...] * lk1_ref[...])) - jnp.exp(jnp.sum(lq2_ref[...] * lk2_ref[...]))
           + lam_init)

    for g in range(nq):
        qt = qt_ref[0, :, g * blk:(g + 1) * blk]
        sub = lax.broadcasted_iota(jnp.int32, qt.shape, 0)
        zero = jnp.zeros_like(qt)
        qq_ref[g, :, 0:blk] = jnp.where(sub < DIFF_QK_DIM, qt, zero)
        qq_ref[g, :, blk:2 * blk] = jnp.where(sub >= DIFF_QK_DIM, qt, zero)

    def query_block(j):
        return pl.program_id(2) * nq + j // nblk

    def key_block(j):
        kb = query_block(j) - 1 + j % nblk
        kb = jnp.where(kb < 0, kb + nblk, kb)
        return jnp.where(kb >= nblk, kb - nblk, kb)

    def near(j):
        return j % nblk < NEAR_BLOCKS

    def scores(j):
        k0 = pl.multiple_of(key_block(j) * blk, blk)
        for r in range(nstrip):
            s_ref[j % 3, r * strip:(r + 1) * strip, :] = _dot(k_ref[0, 0, pl.ds(k0 + r * strip, strip), :],
                                                               qq_ref[j // nblk])

    def biased(j, r):
        rows = slice(r * strip, (r + 1) * strip)
        s = s_ref[j % 3, rows, :]
        if near(j):
            b = bias_ref[0, jnp.clip(key_block(j) - query_block(j), -2, 2) + 2, rows, :]
            s = s + jnp.concatenate([b, b], axis=1)
        return s

    def stats(j):
        g = j // nblk
        mx8 = jnp.full((8, 2 * blk), -jnp.inf, F32)
        for r in range(nstrip):
            s = biased(j, r)
            for i in range(strip // 8):
                mx8 = jnp.maximum(mx8, s[i * 8:(i + 1) * 8])
        shift = 0.0 if near(j) else jnp.where(key_block(j) < query_block(j), far_left, far_right)
        m_blk = jnp.max(mx8, axis=0, keepdims=True) + shift
        if j % nblk == 0:
            m_new = m_blk
        else:
            m_prev = m_ref[g]
            m_new = jnp.maximum(m_prev, m_blk)
            st_ref[j % 4, 0:1, :] = jnp.exp2(m_prev - m_new)
        m_ref[g] = m_new
        st_ref[j % 4, 1:2, :] = m_new - shift

    def probs(j):
        off = st_ref[j % 4, 1:2, :]
        for r in range(nstrip):
            p_ref[j % 2, r * strip:(r + 1) * strip, :] = jnp.exp2(biased(j, r) - off).astype(BF16)

    def values(j):
        g = j // nblk
        k0 = pl.multiple_of(key_block(j) * blk, blk)
        vta = jnp.concatenate([vt_ref[0, :, pl.ds(k0, blk)], ones], axis=0)
        pv = _dot(vta, p_ref[j % 2])
        acc_ref[g] = pv if j % nblk == 0 else st_ref[j % 4, 0:1, :] * acc_ref[g] + pv
        if j % nblk == nblk - 1:
            acc = acc_ref[g]
            ot = acc[:HEAD_DIM] / acc[HEAD_DIM:HEAD_DIM + 1]
            o = (ot[:, :blk] - lam * ot[:, blk:]).T
            o_ref[0, g * blk:(g + 1) * blk, :] = (_rms(o, gd_ref[...]) * (1.0 - lam_init)).astype(BF16)

    njobs = nq * nblk
    for stage in range(njobs + 3):
        for step, lag in ((values, 3), (scores, 0), (stats, 1), (probs, 2)):
            if 0 <= stage - lag < njobs:
                step(stage - lag)


def _diff_attn(rel_bias, qt, k, vt, bias, lq1, lk1, lq2, lk2, g_diff, nh, lam_init):
    b, _, t, _ = k.shape
    blk = ATT_BLOCK
    nq = ATT_QBLOCKS
    assert t // blk >= NEAR_BLOCKS and (t // blk) % nq == 0
    small = lambda bi, h, qi: (0, 0)
    return pl.pallas_call(
        functools.partial(_attn_kernel, blk=blk, strip=ATT_STRIP, seq=t, nq=nq, lam_init=lam_init),
        grid=(b, nh, t // (nq * blk)),
        in_specs=[pl.BlockSpec(memory_space=pltpu.SMEM),
                  pl.BlockSpec((1, HEAD_DIM, nq * blk), lambda bi, h, qi: (bi, h, qi)),
                  pl.BlockSpec((1, 1, t, HEAD_DIM), lambda bi, h, qi: (bi, h, 0, 0)),
                  pl.BlockSpec((1, HEAD_DIM, t), lambda bi, h, qi: (bi, h, 0)),
                  pl.BlockSpec((1, 5, blk, blk), lambda bi, h, qi: (h, 0, 0, 0)),
                  pl.BlockSpec((1, DIFF_QK_DIM), small), pl.BlockSpec((1, DIFF_QK_DIM), small),
                  pl.BlockSpec((1, DIFF_QK_DIM), small), pl.BlockSpec((1, DIFF_QK_DIM), small),
                  pl.BlockSpec((1, HEAD_DIM), small)],
        out_specs=pl.BlockSpec((1, nq * blk, HEAD_DIM), lambda bi, h, qi: (bi, qi, h)),
        out_shape=jax.ShapeDtypeStruct((b, t, nh * HEAD_DIM), BF16),
        scratch_shapes=[pltpu.VMEM((nq, HEAD_DIM, 2 * blk), BF16),
                        pltpu.VMEM((nq, 1, 2 * blk), F32),
                        pltpu.VMEM((4, 2, 2 * blk), F32),
                        pltpu.VMEM((nq, HEAD_DIM + ONES_ROWS, 2 * blk), F32),
                        pltpu.VMEM((3, blk, 2 * blk), F32),
                        pltpu.VMEM((2, blk, 2 * blk), BF16)],
        compiler_params=_cparams(("parallel", "parallel", "arbitrary")),
        name="diff_attn",
    )(rel_bias, qt, k, vt, bias, lq1, lk1, lq2, lk2, g_diff)


def _hgrn_masks(rev):
    c = HGRN_CHUNK
    row = lax.broadcasted_iota(jnp.int32, (c, c), 0)
    col = lax.broadcasted_iota(jnp.int32, (c, c), 1)
    tri = (col >= row) if rev else (col <= row)
    masks = []
    b = HGRN_BASE
    while b < c:
        masks.append((b, ((row ^ col) >> int(math.log2(b))) == 1))
        b *= 2
    sh = int(math.log2(HGRN_BASE))
    diag = ((row >> sh) == (col >> sh)) & tri
    return tri.astype(BF16), masks, diag


def _row_refs(cum, idxs, rows_each):
    return jnp.concatenate(
        [jnp.broadcast_to(cum[i:i + 1, :], (rows_each, cum.shape[1])) for i in idxs], axis=0)


def _hgrn_chunks(q, kk, logf, vs, vts, st, tri, masks, diag, rev):
    c = HGRN_CHUNK
    n = len(vs)
    d = q.shape[1] // n
    part = lambda x, i: x[:, i * d:(i + 1) * d]
    top = lax.bitcast_convert_type(lax.bitcast_convert_type(logf, jnp.int32) & TOP16, F32)
    cum = _dot(tri, top.astype(BF16)) + _dot(tri, (logf - top).astype(BF16))
    last = cum[0:1, :] if rev else cum[c - 1:c, :]
    rowi = lax.broadcasted_iota(jnp.int32, q.shape, 0)

    a = [jnp.zeros((c, c), F32)] * n
    for b, siblings in masks:
        ngroups = c // (2 * b)
        ref = _row_refs(cum, [g * 2 * b + (b if rev else b - 1) for g in range(ngroups)], 2 * b)
        first_half = (rowi & (2 * b - 1)) < b
        is_q = first_half if rev else jnp.logical_not(first_half)
        qt = jnp.where(is_q, q * jnp.exp(jnp.minimum(cum - ref, 0.0)), 0.0).astype(BF16)
        kt = jnp.where(is_q, 0.0, kk * jnp.exp(jnp.minimum(ref - cum, 0.0))).astype(BF16)
        a = [jnp.where(siblings, _dot_nt(part(qt, i), part(kt, i)), a[i]) for i in range(n)]
    nb = c // HGRN_BASE
    ref = _row_refs(cum, [g * HGRN_BASE + HGRN_BASE // 2 for g in range(nb)], HGRN_BASE)
    qd = (q * jnp.exp(cum - ref)).astype(BF16)
    kd = (kk * jnp.exp(ref - cum)).astype(BF16)
    a = [jnp.where(diag, _dot_nt(part(qd, i), part(kd, i)), a[i]) for i in range(n)]

    qe = (q * jnp.exp(cum)).astype(BF16)
    kl = (kk * jnp.exp(last - cum)).astype(BF16)
    decay = jnp.exp(last)
    intra = [_dot(a[i].astype(BF16), vs[i]) for i in range(n)]
    update = [_dot(vts[i], part(kl, i)) for i in range(n)]
    outs = [None] * n
    for i in (range(n - 1, -1, -1) if rev else range(n)):
        outs[i] = intra[i] + _dot_nt(part(qe, i), st.astype(BF16))
        st = st * part(decay, i) + update[i]
    return outs, st


def _hgrn_kernel(*refs, rev, nchunk, level):
    if rev:
        q_ref, v_ref, vt_ref, f_ref, lb_ref, of_ref, g_ref, gn_ref, o_ref, st_ref = refs
    else:
        q_ref, v_ref, vt_ref, f_ref, lb_ref, o_ref, st_ref = refs

    @pl.when(pl.program_id(2) == 0)
    def _():
        st_ref[...] = jnp.zeros(st_ref.shape, F32)

    lp = lb_ref[0]
    e = jnp.exp(lp - jnp.max(lp, axis=0, keepdims=True))
    lb = jnp.sum(e[0:level + 1], axis=0, keepdims=True) / jnp.sum(e, axis=0, keepdims=True)

    tri, masks, diag = _hgrn_masks(rev)
    c = HGRN_CHUNK
    chunks = [slice(ci * c, (ci + 1) * c) for ci in range(nchunk)]
    wide = lambda ref: jnp.concatenate([ref[0, rows, :] for rows in chunks], axis=1)
    lbw = jnp.concatenate([lb] * nchunk, axis=1)
    f = lbw + (1.0 - lbw) * _sigmoid(wide(f_ref))
    outs, st = _hgrn_chunks(wide(q_ref).astype(F32), 1.0 - f, jnp.log(f),
                            [v_ref[0, rows, :] for rows in chunks], [vt_ref[0, :, rows] for rows in chunks],
                            st_ref[...], tri, masks, diag, rev)
    st_ref[...] = st
    for rows, out in zip(chunks, outs):
        if rev:
            o = out + of_ref[0, rows, :]
            o_ref[0, rows, :] = (_rms(o, gn_ref[...]) * g_ref[0, rows, :]).astype(BF16)
        else:
            o_ref[0, rows, :] = out


def _hgrn_dir(hq, hi, hit, hf, lb_param, nh, rev, level, extra=()):
    b, t, _ = hq.shape
    step = min(HGRN_STEP, t)
    nstep = t // step
    d = HEAD_DIM
    tpos = (lambda n: nstep - 1 - n) if rev else (lambda n: n)
    row = lambda bi, h, n: (bi, tpos(n), h)
    in_specs = [pl.BlockSpec((1, step, d), row),
                pl.BlockSpec((1, step, d), row),
                pl.BlockSpec((1, d, step), lambda bi, h, n: (bi, h, tpos(n))),
                pl.BlockSpec((1, step, d), lambda bi, h, n: (bi, tpos(n), (nh if rev else 0) + h)),
                pl.BlockSpec((1, lb_param.shape[1], d), lambda bi, h, n: (1 if rev else 0, 0, h))]
    if rev:
        in_specs += [pl.BlockSpec((1, step, d), row), pl.BlockSpec((1, step, d), row),
                     pl.BlockSpec((1, d), lambda bi, h, n: (0, 0))]
    return pl.pallas_call(
        functools.partial(_hgrn_kernel, rev=rev, nchunk=step // HGRN_CHUNK, level=level),
        grid=(b, nh, nstep),
        in_specs=in_specs,
        out_specs=pl.BlockSpec((1, step, d), row),
        out_shape=jax.ShapeDtypeStruct((b, t, nh * d), BF16 if rev else F32),
        scratch_shapes=[pltpu.VMEM((d, d), F32)],
        compiler_params=_cparams(("parallel", "parallel", "arbitrary")),
        name="hgrn_bwd" if rev else "hgrn_fwd",
    )(hq, hi, hit, hf, lb_param, *extra)


def _out_proj_kernel(a_ref, o_ref, x_ref, w_ref, gp_ref, gf_ref, h_ref, u_ref, *, wd):
    m = _dot(a_ref[...], w_ref[0:wd, :]) + _dot(o_ref[...], w_ref[wd:, :])
    h = x_ref[...] + _rms(m, gp_ref[...])
    h_ref[...] = h
    u_ref[...] = _rms(h, gf_ref[...]).astype(BF16)


def _out_proj(a, o, x, w, g_post, g_ffn):
    n, d = x.shape
    wd = a.shape[1]
    tm = TOKEN_TILE
    row = lambda i: (i, 0)
    fixed = lambda i: (0, 0)
    return pl.pallas_call(
        functools.partial(_out_proj_kernel, wd=wd),
        grid=(n // tm,),
        in_specs=[pl.BlockSpec((tm, wd), row), pl.BlockSpec((tm, o.shape[1]), row), pl.BlockSpec((tm, d), row),
                  pl.BlockSpec(w.shape, fixed), pl.BlockSpec((1, d), fixed), pl.BlockSpec((1, d), fixed)],
        out_specs=[pl.BlockSpec((tm, d), row), pl.BlockSpec((tm, d), row)],
        out_shape=[jax.ShapeDtypeStruct((n, d), F32), jax.ShapeDtypeStruct((n, d), BF16)],
        compiler_params=_cparams(("parallel",)),
        name="out_proj",
    )(a, o, x, w, g_post, g_ffn)


def _ffn_kernel(u_ref, up_ref, un_ref, wg_ref, wu_ref, cwg_ref, cwu_ref, cbg_ref, cbu_ref, wd_ref,
                h_ref, g_ref, o_ref, acc_ref, sh_ref, *, tiles_per_seq, sub):
    i = pl.program_id(0)
    j = pl.program_id(1)
    nj = pl.num_programs(1)
    tm = u_ref.shape[0]

    @pl.when(j == 0)
    def _():
        acc_ref[...] = jnp.zeros(acc_ref.shape, F32)

    u = u_ref[...]
    halo = jnp.concatenate([up_ref[...], un_ref[...]], axis=0)
    ti = i % tiles_per_seq
    has_prev = jnp.where(ti > 0, 1.0, 0.0)
    has_next = jnp.where(ti < tiles_per_seq - 1, 1.0, 0.0)
    c0 = math.sqrt(2.0 / math.pi)
    c1 = c0 * 0.044715
    lane_tiles = sub // LANES
    pad = SUBLANES

    def project(c):
        cols = slice(c * sub, (c + 1) * sub)
        return [(_dot(u, w_ref[:, cols]), _dot(halo, w_ref[:, cols])) for w_ref in (wg_ref, wu_ref)]

    def conv(c, branch, a, ah, cw_ref, cb_ref):
        outs = []
        for lt in range(lane_tiles):
            slab = ((c % 2) * 2 + branch) * lane_tiles + lt
            lanes = slice(lt * LANES, (lt + 1) * LANES)
            cols = slice(c * sub + lt * LANES, c * sub + (lt + 1) * LANES)
            al = a[:, lanes]
            sh_ref[slab, pad - 1:pad, :] = ah[HALO_ROWS - 1:HALO_ROWS, lanes] * has_prev
            sh_ref[slab, pad:pad + tm, :] = al
            sh_ref[slab, pad + tm:pad + tm + 1, :] = ah[HALO_ROWS:HALO_ROWS + 1, lanes] * has_next
            cw = cw_ref[:, cols]
            outs.append(cw[0:1, :] * sh_ref[slab, pad - 1:pad - 1 + tm, :] + cw[1:2, :] * al
                        + cw[2:3, :] * sh_ref[slab, pad + 1:pad + 1 + tm, :] + cb_ref[:, cols])
        return jnp.concatenate(outs, axis=1)

    def activate(c, projected):
        gate, up = [conv(c, br, a, ah, cw_ref, cb_ref) for br, ((a, ah), cw_ref, cb_ref)
                    in enumerate(zip(projected, (cwg_ref, cwu_ref), (cbg_ref, cbu_ref)))]
        return (gate * up * (1.0 + jnp.tanh(gate * (c0 + c1 * (gate * gate))))).astype(BF16)

    nsub = wg_ref.shape[1] // sub
    projected = [project(c) for c in range(min(2, nsub))]
    for c in range(nsub):
        if c + 2 < nsub:
            projected.append(project(c + 2))
        acc_ref[...] += _dot(activate(c, projected[c]), wd_ref[c * sub:(c + 1) * sub, :])

    @pl.when(j == nj - 1)
    def _():
        o_ref[...] = h_ref[...] + _rms(acc_ref[...], g_ref[...])


def _conv_ffn(u, h, w_up, conv_w, conv_b, w_down, g_post, seq):
    n, d = h.shape
    dff = w_down.shape[0]
    tm = min(FFN_TOKEN_TILE, seq)
    tf = FFN_COL_TILE
    nf = dff // tf
    hb = tm // HALO_ROWS
    nhalo = n // HALO_ROWS
    row = lambda i, j: (i, 0)
    return pl.pallas_call(
        functools.partial(_ffn_kernel, tiles_per_seq=seq // tm, sub=FFN_SUB_TILE),
        grid=(n // tm, nf),
        in_specs=[pl.BlockSpec((tm, d), row),
                  pl.BlockSpec((HALO_ROWS, d), lambda i, j: (jnp.maximum(i * hb - 1, 0), 0)),
                  pl.BlockSpec((HALO_ROWS, d), lambda i, j: (jnp.minimum((i + 1) * hb, nhalo - 1), 0)),
                  pl.BlockSpec((d, tf), lambda i, j: (0, j)),
                  pl.BlockSpec((d, tf), lambda i, j: (0, nf + j)),
                  pl.BlockSpec((CONV_WIDTH, tf), lambda i, j: (0, j)),
                  pl.BlockSpec((CONV_WIDTH, tf), lambda i, j: (0, nf + j)),
                  pl.BlockSpec((1, tf), lambda i, j: (0, j)),
                  pl.BlockSpec((1, tf), lambda i, j: (0, nf + j)),
                  pl.BlockSpec((tf, d), lambda i, j: (j, 0)),
                  pl.BlockSpec((tm, d), row),
                  pl.BlockSpec((1, d), lambda i, j: (0, 0))],
        out_specs=pl.BlockSpec((tm, d), row),
        out_shape=jax.ShapeDtypeStruct((n, d), F32),
        scratch_shapes=[pltpu.VMEM((tm, d), F32),
                        pltpu.VMEM((4 * FFN_SUB_TILE // LANES, tm + 2 * SUBLANES, LANES), F32)],
        compiler_params=_cparams(("parallel", "arbitrary")),
        name="conv_ffn",
    )(u, u, u, w_up, w_up, conv_w, conv_w, conv_b, conv_b, w_down, h, g_post)


def _ple_kernel(h_ref, p_ref, wp_ref, wg_ref, g_ref, o_ref):
    h = h_ref[...]
    e = _rms(_dot(p_ref[...].astype(BF16), wp_ref[...]), g_ref[...])
    gate = _sigmoid(_dot(h.astype(BF16), wg_ref[...]))
    o_ref[...] = h + e * gate


def _ple(h, p, w_ple, w_gate, g):
    n, d = h.shape
    tm = TOKEN_TILE
    row = lambda i: (i, 0)
    fixed = lambda i: (0, 0)
    return pl.pallas_call(
        _ple_kernel,
        grid=(n // tm,),
        in_specs=[pl.BlockSpec((tm, d), row), pl.BlockSpec((tm, p.shape[1]), row),
                  pl.BlockSpec(w_ple.shape, fixed), pl.BlockSpec(w_gate.shape, fixed), pl.BlockSpec((1, d), fixed)],
        out_specs=pl.BlockSpec((tm, d), row),
        out_shape=jax.ShapeDtypeStruct((n, d), F32),
        compiler_params=_cparams(("parallel",)),
        name="ple",
    )(h, p, w_ple, w_gate, g)


def kernel(x, p, rel_bias, g_pre_mix, w_in, lambda_q1, lambda_k1, lambda_q2, lambda_k2, g_diff, lb_param, g_hgrn, w_out, g_post_mix, g_pre_ffn, w_up, conv_w, conv_b, w_down, g_post_ffn, w_ple, g_ple, w_ple_gate):
    b, t, d = x.shape
    depth = w_in.shape[0]
    wd = d // 2
    wh = d - wd
    nh_diff = wd // HEAD_DIM
    nh_hgrn = wh // HEAD_DIM
    assert w_in.shape[2] == 3 * wd + 5 * wh
    assert t % ATT_BLOCK == 0 and t % HGRN_CHUNK == 0 and (b * t) % TOKEN_TILE == 0

    rb = rel_bias.astype(F32)
    bias = _rel_bias_blocks(rb, ATT_BLOCK)
    h = x
    for l in range(depth):
        vec = lambda a: a[l].reshape(1, -1).astype(F32)
        hid = h
        qt, k, vt, hq, hi, hit, hf, hg = _in_proj(hid, vec(g_pre_mix), w_in[l].astype(BF16), wd, wh)
        lam_init = 0.8 - 0.6 * math.exp(-0.3 * l)
        a = _diff_attn(rb, qt, k, vt, bias, vec(lambda_q1), vec(lambda_k1), vec(lambda_q2), vec(lambda_k2),
                       vec(g_diff), nh_diff, lam_init)
        lbp = lb_param.astype(F32)
        o_fwd = _hgrn_dir(hq, hi, hit, hf, lbp, nh_hgrn, False, l)
        o = _hgrn_dir(hq, hi, hit, hf, lbp, nh_hgrn, True, l, extra=(o_fwd, hg, vec(g_hgrn)))
        n = b * t
        h1, u2 = _out_proj(a.reshape(n, wd), o.reshape(n, wh), hid.reshape(n, d), w_out[l].astype(BF16),
                           vec(g_post_mix), vec(g_pre_ffn))
        h2 = _conv_ffn(u2, h1, w_up[l].astype(BF16), conv_w[l].astype(F32), conv_b[l].reshape(1, -1).astype(F32),
                       (0.5 * w_down[l]).astype(BF16), vec(g_post_ffn), t)
        h3 = _ple(h2, p[l].reshape(n, -1), w_ple[l].astype(BF16), w_ple_gate[l].astype(BF16), vec(g_ple))
        h = h3.reshape(b, t, d)
    return h
```

```python
import functools
import math

import jax
import jax.numpy as jnp
from jax import lax
from jax.experimental import pallas as pl
from jax.experimental.pallas import tpu as pltpu

F32 = jnp.float32
BF16 = jnp.bfloat16

EPS = 1e-6
SUBLANES = 8
LANES = 128
DIFF_QK_DIM = 64
HEAD_DIM = 128
REL_BUCKETS = 32
REL_MAX_DIST = 128
CONV_WIDTH = 3

ATT_BLOCK = 512
ATT_STRIP = 512
ATT_QBLOCKS = 2
NEAR_BLOCKS = 3
ONES_ROWS = 16
LOG2E = math.log2(math.e)
TOP16 = -65536
HGRN_CHUNK = 128
HGRN_BASE = 16
HGRN_STEP = 1024
TOKEN_TILE = 512
WIDE_TOKEN_TILE = 1024
FFN_TOKEN_TILE = 1024
FFN_COL_TILE = 1024
FFN_SUB_TILE = 256
FFN_LOOKAHEAD_SETS = 3
HALO_ROWS = 16
VMEM_LIMIT = 56 * 1024 * 1024


def _cparams(sem, flags=None):
    return pltpu.CompilerParams(dimension_semantics=sem, vmem_limit_bytes=VMEM_LIMIT, flags=flags)


def _rms(x, g):
    return x * lax.rsqrt(jnp.mean(x * x, axis=-1, keepdims=True) + EPS) * g


def _sigmoid(x):
    return 1.0 / (1.0 + jnp.exp(-x))


def _dot(a, b):
    return jnp.dot(a, b, preferred_element_type=F32)


def _dot_nt(a, b):
    return lax.dot_general(a, b, (((1,), (1,)), ((), ())), preferred_element_type=F32)


def _log_bucket_thresholds():
    half = REL_BUCKETS // 2
    max_exact = half // 2
    nlog = half - max_exact
    thr = []
    for k in range(1, nlog):
        n = max_exact
        while (n ** nlog) * (max_exact ** k) < (REL_MAX_DIST ** k) * (max_exact ** nlog):
            n += 1
        thr.append(n)
    return max_exact, half, thr


def _bias_kernel(rb_ref, o_ref, *, blk):
    h = pl.program_id(0)
    d = pl.program_id(1) - 2
    max_exact, half, thr = _log_bucket_thresholds()
    width = 2 * blk
    z = lax.broadcasted_iota(jnp.int32, (SUBLANES, width), 1)
    z = jnp.where(z < blk, z, z - width)
    rel = d * blk - z
    n = jnp.abs(rel)
    large = jnp.full(rel.shape, max_exact, jnp.int32)
    for t in thr:
        large = large + (n >= t).astype(jnp.int32)
    bucket = jnp.where(rel > 0, half, 0) + jnp.where(n < max_exact, n, large)
    bias = jnp.zeros(rel.shape, F32)
    for b in range(REL_BUCKETS):
        bias = jnp.where(bucket == b, rb_ref[b, h], bias)
    rows = jnp.broadcast_to(bias[0:1, :] * LOG2E, (blk, width))
    o_ref[0, 0] = pltpu.roll(rows, 0, 1, stride=1, stride_axis=0)[:, :blk]


def _rel_bias_blocks(rel_bias, blk):
    assert blk >= REL_MAX_DIST
    nh = rel_bias.shape[1]
    return pl.pallas_call(
        functools.partial(_bias_kernel, blk=blk),
        grid=(nh, 5),
        in_specs=[pl.BlockSpec(memory_space=pltpu.SMEM)],
        out_specs=pl.BlockSpec((1, 1, blk, blk), lambda h, d: (h, d, 0, 0)),
        out_shape=jax.ShapeDtypeStruct((nh, 5, blk, blk), F32),
        compiler_params=_cparams(("parallel", "parallel")),
        name="rel_bias_blocks",
    )(rel_bias)


def _in_proj_kernel(x_ref, g_ref, w_ref, qt_ref, k_ref, vt_ref, hq_ref, hi_ref, hit_ref, hf_ref, hg_ref,
                    *, wd, wh):
    u = _rms(x_ref[0], g_ref[...]).astype(BF16)

    def proj(a, b):
        return _dot(u, w_ref[:, a:b])

    qt_ref[0] = (proj(0, wd) * (DIFF_QK_DIM ** -0.5 * LOG2E)).T.astype(BF16)
    kp = proj(wd, 2 * wd).astype(BF16)
    for h in range(wd // HEAD_DIM):
        k_ref[0, h] = kp[:, h * HEAD_DIM:(h + 1) * HEAD_DIM]
    vt_ref[0] = proj(2 * wd, 3 * wd).T.astype(BF16)
    o = 3 * wd
    qh = proj(o, o + wh)
    hq_ref[0] = (qh * _sigmoid(qh)).astype(BF16)
    ih = proj(o + wh, o + 2 * wh)
    hi_ref[0] = ih.astype(BF16)
    hit_ref[0] = ih.T.astype(BF16)
    hf_ref[0] = proj(o + 2 * wh, o + 4 * wh)
    gh = proj(o + 4 * wh, o + 5 * wh)
    hg_ref[0] = gh * _sigmoid(gh)


def _in_proj(x, g, w, wd, wh):
    b, t, d = x.shape
    tm = TOKEN_TILE
    ncol = w.shape[1]
    row = lambda bi, ti: (bi, ti, 0)
    col = lambda bi, ti: (bi, 0, ti)
    return pl.pallas_call(
        functools.partial(_in_proj_kernel, wd=wd, wh=wh),
        grid=(b, t // tm),
        in_specs=[pl.BlockSpec((1, tm, d), row),
                  pl.BlockSpec((1, d), lambda bi, ti: (0, 0)),
                  pl.BlockSpec((d, ncol), lambda bi, ti: (0, 0))],
        out_specs=[pl.BlockSpec((1, wd, tm), col),
                   pl.BlockSpec((1, wd // HEAD_DIM, tm, HEAD_DIM), lambda bi, ti: (bi, 0, ti, 0)),
                   pl.BlockSpec((1, wd, tm), col),
                   pl.BlockSpec((1, tm, wh), row),
                   pl.BlockSpec((1, tm, wh), row),
                   pl.BlockSpec((1, wh, tm), col),
                   pl.BlockSpec((1, tm, 2 * wh), row),
                   pl.BlockSpec((1, tm, wh), row)],
        out_shape=[jax.ShapeDtypeStruct((b, wd, t), BF16),
                   jax.ShapeDtypeStruct((b, wd // HEAD_DIM, t, HEAD_DIM), BF16),
                   jax.ShapeDtypeStruct((b, wd, t), BF16),
                   jax.ShapeDtypeStruct((b, t, wh), BF16),
                   jax.ShapeDtypeStruct((b, t, wh), BF16),
                   jax.ShapeDtypeStruct((b, wh, t), BF16),
                   jax.ShapeDtypeStruct((b, t, 2 * wh), F32),
                   jax.ShapeDtypeStruct((b, t, wh), F32)],
        compiler_params=_cparams(("parallel", "parallel")),
        name="in_proj",
    )(x, g, w)


def _attn_kernel(rb_ref, qt_ref, k_ref, vt_ref, bias_ref, lq1_ref, lk1_ref, lq2_ref, lk2_ref, gd_ref,
                 o_ref, qq_ref, m_ref, st_ref, acc_ref, s_ref, p_ref, *, blk, strip, seq, nq, lam_init):
    h = pl.program_id(1)
    nblk = seq // blk
    nstrip = blk // strip
    ones = jnp.ones((ONES_ROWS, blk), BF16)
    half = REL_BUCKETS // 2
    far_left = rb_ref[half - 1, h] * LOG2E
    far_right = rb_ref[2 * half - 1, h] * LOG2E
    lam = (jnp.exp(jnp.sum(lq1_ref[...] * lk1_ref[...])) - jnp.exp(jnp.sum(lq2_ref[...] * lk2_ref[...]))
           + lam_init)

    for g in range(nq):
        qt = qt_ref[0, :, g * blk:(g + 1) * blk]
        sub = lax.broadcasted_iota(jnp.int32, qt.shape, 0)
        zero = jnp.zeros_like(qt)
        qq_ref[g, :, 0:blk] = jnp.where(sub < DIFF_QK_DIM, qt, zero)
        qq_ref[g, :, blk:2 * blk] = jnp.where(sub >= DIFF_QK_DIM, qt, zero)

    def query_block(j):
        return pl.program_id(2) * nq + j // nblk

    def key_block(j):
        kb = query_block(j) - 1 + j % nblk
        kb = jnp.where(kb < 0, kb + nblk, kb)
        return jnp.where(kb >= nblk, kb - nblk, kb)

    def near(j):
        return j % nblk < NEAR_BLOCKS

    def scores(j):
        k0 = pl.multiple_of(key_block(j) * blk, blk)
        for r in range(nstrip):
            s_ref[j % 3, r * strip:(r + 1) * strip, :] = _dot(k_ref[0, 0, pl.ds(k0 + r * strip, strip), :],
                                                               qq_ref[j // nblk])

    def biased(j, r):
        rows = slice(r * strip, (r + 1) * strip)
        s = s_ref[j % 3, rows, :]
        if near(j):
            b = bias_ref[0, jnp.clip(key_block(j) - query_block(j), -2, 2) + 2, rows, :]
            s = s + jnp.concatenate([b, b], axis=1)
        return s

    def stats(j):
        g = j // nblk
        mx8 = jnp.full((8, 2 * blk), -jnp.inf, F32)
        for r in range(nstrip):
            s = biased(j, r)
            for i in range(strip // 8):
                mx8 = jnp.maximum(mx8, s[i * 8:(i + 1) * 8])
        shift = 0.0 if near(j) else jnp.where(key_block(j) < query_block(j), far_left, far_right)
        m_blk = jnp.max(mx8, axis=0, keepdims=True) + shift
        if j % nblk == 0:
            m_new = m_blk
        else:
            m_prev = m_ref[g]
            m_new = jnp.maximum(m_prev, m_blk)
            st_ref[j % 4, 0:1, :] = jnp.exp2(m_prev - m_new)
        m_ref[g] = m_new
        st_ref[j % 4, 1:2, :] = m_new - shift

    def probs(j):
        off = st_ref[j % 4, 1:2, :]
        for r in range(nstrip):
            p_ref[j % 2, r * strip:(r + 1) * strip, :] = jnp.exp2(biased(j, r) - off).astype(BF16)

    def values(j):
        g = j // nblk
        k0 = pl.multiple_of(key_block(j) * blk, blk)
        vta = jnp.concatenate([vt_ref[0, :, pl.ds(k0, blk)], ones], axis=0)
        pv = _dot(vta, p_ref[j % 2])
        acc_ref[g] = pv if j % nblk == 0 else st_ref[j % 4, 0:1, :] * acc_ref[g] + pv
        if j % nblk == nblk - 1:
            acc = acc_ref[g]
            ot = acc[:HEAD_DIM] / acc[HEAD_DIM:HEAD_DIM + 1]
            o = (ot[:, :blk] - lam * ot[:, blk:]).T
            o_ref[0, g * blk:(g + 1) * blk, :] = (_rms(o, gd_ref[...]) * (1.0 - lam_init)).astype(BF16)

    njobs = nq * nblk
    for stage in range(njobs + 3):
        for step, lag in ((values, 3), (scores, 0), (stats, 1), (probs, 2)):
            if 0 <= stage - lag < njobs:
                step(stage - lag)


def _diff_attn(rel_bias, qt, k, vt, bias, lq1, lk1, lq2, lk2, g_diff, nh, lam_init):
    b, _, t, _ = k.shape
    blk = ATT_BLOCK
    nq = ATT_QBLOCKS
    assert t // blk >= NEAR_BLOCKS and (t // blk) % nq == 0
    small = lambda bi, h, qi: (0, 0)
    return pl.pallas_call(
        functools.partial(_attn_kernel, blk=blk, strip=ATT_STRIP, seq=t, nq=nq, lam_init=lam_init),
        grid=(b, nh, t // (nq * blk)),
        in_specs=[pl.BlockSpec(memory_space=pltpu.SMEM),
                  pl.BlockSpec((1, HEAD_DIM, nq * blk), lambda bi, h, qi: (bi, h, qi)),
                  pl.BlockSpec((1, 1, t, HEAD_DIM), lambda bi, h, qi: (bi, h, 0, 0)),
                  pl.BlockSpec((1, HEAD_DIM, t), lambda bi, h, qi: (bi, h, 0)),
                  pl.BlockSpec((1, 5, blk, blk), lambda bi, h, qi: (h, 0, 0, 0)),
                  pl.BlockSpec((1, DIFF_QK_DIM), small), pl.BlockSpec((1, DIFF_QK_DIM), small),
                  pl.BlockSpec((1, DIFF_QK_DIM), small), pl.BlockSpec((1, DIFF_QK_DIM), small),
                  pl.BlockSpec((1, HEAD_DIM), small)],
        out_specs=pl.BlockSpec((1, nq * blk, HEAD_DIM), lambda bi, h, qi: (bi, qi, h)),
        out_shape=jax.ShapeDtypeStruct((b, t, nh * HEAD_DIM), BF16),
        scratch_shapes=[pltpu.VMEM((nq, HEAD_DIM, 2 * blk), BF16),
                        pltpu.VMEM((nq, 1, 2 * blk), F32),
                        pltpu.VMEM((4, 2, 2 * blk), F32),
                        pltpu.VMEM((nq, HEAD_DIM + ONES_ROWS, 2 * blk), F32),
                        pltpu.VMEM((3, blk, 2 * blk), F32),
                        pltpu.VMEM((2, blk, 2 * blk), BF16)],
        compiler_params=_cparams(("parallel", "parallel", "arbitrary")),
        name="diff_attn",
    )(rel_bias, qt, k, vt, bias, lq1, lk1, lq2, lk2, g_diff)


def _hgrn_masks(rev):
    c = HGRN_CHUNK
    row = lax.broadcasted_iota(jnp.int32, (c, c), 0)
    col = lax.broadcasted_iota(jnp.int32, (c, c), 1)
    tri = (col >= row) if rev else (col <= row)
    masks = []
    b = HGRN_BASE
    while b < c:
        masks.append((b, ((row ^ col) >> int(math.log2(b))) == 1))
        b *= 2
    sh = int(math.log2(HGRN_BASE))
    diag = ((row >> sh) == (col >> sh)) & tri
    return tri.astype(BF16), masks, diag


def _row_refs(cum, idxs, rows_each):
    return jnp.concatenate(
        [jnp.broadcast_to(cum[i:i + 1, :], (rows_each, cum.shape[1])) for i in idxs], axis=0)


def _hgrn_chunks(q, kk, logf, vs, vts, st, tri, masks, diag, rev):
    c = HGRN_CHUNK
    n = len(vs)
    d = q.shape[1] // n
    part = lambda x, i: x[:, i * d:(i + 1) * d]
    top = lax.bitcast_convert_type(lax.bitcast_convert_type(logf, jnp.int32) & TOP16, F32)
    cum = _dot(tri, top.astype(BF16)) + _dot(tri, (logf - top).astype(BF16))
    last = cum[0:1, :] if rev else cum[c - 1:c, :]
    rowi = lax.broadcasted_iota(jnp.int32, q.shape, 0)

    a = [jnp.zeros((c, c), F32)] * n
    for b, siblings in masks:
        ngroups = c // (2 * b)
        ref = _row_refs(cum, [g * 2 * b + (b if rev else b - 1) for g in range(ngroups)], 2 * b)
        first_half = (rowi & (2 * b - 1)) < b
        is_q = first_half if rev else jnp.logical_not(first_half)
        qt = jnp.where(is_q, q * jnp.exp(jnp.minimum(cum - ref, 0.0)), 0.0).astype(BF16)
        kt = jnp.where(is_q, 0.0, kk * jnp.exp(jnp.minimum(ref - cum, 0.0))).astype(BF16)
        a = [jnp.where(siblings, _dot_nt(part(qt, i), part(kt, i)), a[i]) for i in range(n)]
    nb = c // HGRN_BASE
    ref = _row_refs(cum, [g * HGRN_BASE + HGRN_BASE // 2 for g in range(nb)], HGRN_BASE)
    qd = (q * jnp.exp(cum - ref)).astype(BF16)
    kd = (kk * jnp.exp(ref - cum)).astype(BF16)
    a = [jnp.where(diag, _dot_nt(part(qd, i), part(kd, i)), a[i]) for i in range(n)]

    qe = (q * jnp.exp(cum)).astype(BF16)
    kl = (kk * jnp.exp(last - cum)).astype(BF16)
    decay = jnp.exp(last)
    intra = [_dot(a[i].astype(BF16), vs[i]) for i in range(n)]
    update = [_dot(vts[i], part(kl, i)) for i in range(n)]
    outs = [None] * n
    for i in (range(n - 1, -1, -1) if rev else range(n)):
        outs[i] = intra[i] + _dot_nt(part(qe, i), st.astype(BF16))
        st = st * part(decay, i) + update[i]
    return outs, st


def _hgrn_kernel(*refs, rev, nchunk, level):
    if rev:
        q_ref, v_ref, vt_ref, f_ref, lb_ref, of_ref, g_ref, gn_ref, o_ref, st_ref = refs
    else:
        q_ref, v_ref, vt_ref, f_ref, lb_ref, o_ref, st_ref = refs

    @pl.when(pl.program_id(2) == 0)
    def _():
        st_ref[...] = jnp.zeros(st_ref.shape, F32)

    lp = lb_ref[0]
    e = jnp.exp(lp - jnp.max(lp, axis=0, keepdims=True))
    lb = jnp.sum(e[0:level + 1], axis=0, keepdims=True) / jnp.sum(e, axis=0, keepdims=True)

    tri, masks, diag = _hgrn_masks(rev)
    c = HGRN_CHUNK
    chunks = [slice(ci * c, (ci + 1) * c) for ci in range(nchunk)]
    wide = lambda ref: jnp.concatenate([ref[0, rows, :] for rows in chunks], axis=1)
    lbw = jnp.concatenate([lb] * nchunk, axis=1)
    f = lbw + (1.0 - lbw) * _sigmoid(wide(f_ref))
    outs, st = _hgrn_chunks(wide(q_ref).astype(F32), 1.0 - f, jnp.log(f),
                            [v_ref[0, rows, :] for rows in chunks], [vt_ref[0, :, rows] for rows in chunks],
                            st_ref[...], tri, masks, diag, rev)
    st_ref[...] = st
    for rows, out in zip(chunks, outs):
        if rev:
            o = out + of_ref[0, rows, :]
            o_ref[0, rows, :] = (_rms(o, gn_ref[...]) * g_ref[0, rows, :]).astype(BF16)
        else:
            o_ref[0, rows, :] = out


def _hgrn_dir(hq, hi, hit, hf, lb_param, nh, rev, level, extra=()):
    b, t, _ = hq.shape
    step = min(HGRN_STEP, t)
    nstep = t // step
    d = HEAD_DIM
    tpos = (lambda n: nstep - 1 - n) if rev else (lambda n: n)
    row = lambda bi, h, n: (bi, tpos(n), h)
    in_specs = [pl.BlockSpec((1, step, d), row),
                pl.BlockSpec((1, step, d), row),
                pl.BlockSpec((1, d, step), lambda bi, h, n: (bi, h, tpos(n))),
                pl.BlockSpec((1, step, d), lambda bi, h, n: (bi, tpos(n), (nh if rev else 0) + h)),
                pl.BlockSpec((1, lb_param.shape[1], d), lambda bi, h, n: (1 if rev else 0, 0, h))]
    if rev:
        in_specs += [pl.BlockSpec((1, step, d), row), pl.BlockSpec((1, step, d), row),
                     pl.BlockSpec((1, d), lambda bi, h, n: (0, 0))]
    return pl.pallas_call(
        functools.partial(_hgrn_kernel, rev=rev, nchunk=step // HGRN_CHUNK, level=level),
        grid=(b, nh, nstep),
        in_specs=in_specs,
        out_specs=pl.BlockSpec((1, step, d), row),
        out_shape=jax.ShapeDtypeStruct((b, t, nh * d), BF16 if rev else F32),
        scratch_shapes=[pltpu.VMEM((d, d), F32)],
        compiler_params=_cparams(("parallel", "parallel", "arbitrary")),
        name="hgrn_bwd" if rev else "hgrn_fwd",
    )(hq, hi, hit, hf, lb_param, *extra)


def _out_proj_kernel(a_ref, o_ref, x_ref, w_ref, gp_ref, gf_ref, h_ref, u_ref, *, wd):
    m = _dot(a_ref[...], w_ref[0:wd, :]) + _dot(o_ref[...], w_ref[wd:, :])
    h = x_ref[...] + _rms(m, gp_ref[...])
    h_ref[...] = h
    u_ref[...] = _rms(h, gf_ref[...]).astype(BF16)


def _out_proj(a, o, x, w, g_post, g_ffn):
    n, d = x.shape
    wd = a.shape[1]
    tm = WIDE_TOKEN_TILE
    row = lambda i: (i, 0)
    fixed = lambda i: (0, 0)
    return pl.pallas_call(
        functools.partial(_out_proj_kernel, wd=wd),
        grid=(n // tm,),
        in_specs=[pl.BlockSpec((tm, wd), row), pl.BlockSpec((tm, o.shape[1]), row), pl.BlockSpec((tm, d), row),
                  pl.BlockSpec(w.shape, fixed), pl.BlockSpec((1, d), fixed), pl.BlockSpec((1, d), fixed)],
        out_specs=[pl.BlockSpec((tm, d), row), pl.BlockSpec((tm, d), row)],
        out_shape=[jax.ShapeDtypeStruct((n, d), F32), jax.ShapeDtypeStruct((n, d), BF16)],
        compiler_params=_cparams(("parallel",)),
        name="out_proj",
    )(a, o, x, w, g_post, g_ffn)


def _ffn_kernel(u_ref, up_ref, un_ref, wg_ref, wu_ref, cwg_ref, cwu_ref, cbg_ref, cbu_ref, wd_ref,
                h_ref, g_ref, o_ref, acc_ref, sh_ref, *, tiles_per_seq, sub):
    i = pl.program_id(0)
    j = pl.program_id(1)
    nj = pl.num_programs(1)
    tm = u_ref.shape[0]

    @pl.when(j == 0)
    def _():
        acc_ref[...] = jnp.zeros(acc_ref.shape, F32)

    u = u_ref[...]
    halo = jnp.concatenate([up_ref[...], un_ref[...]], axis=0)
    ti = i % tiles_per_seq
    has_prev = jnp.where(ti > 0, 1.0, 0.0)
    has_next = jnp.where(ti < tiles_per_seq - 1, 1.0, 0.0)
    c0 = math.sqrt(2.0 / math.pi)
    c1 = c0 * 0.044715
    lane_tiles = sub // LANES
    pad = SUBLANES

    def slab(c, branch, lt):
        return ((c % FFN_LOOKAHEAD_SETS) * 2 + branch) * lane_tiles + lt

    def project(c):
        cols = slice(c * sub, (c + 1) * sub)
        for branch, w_ref in enumerate((wg_ref, wu_ref)):
            w = w_ref[:, cols]
            a = _dot(u, w)
            ah = _dot(halo, w)
            for lt in range(lane_tiles):
                lanes = slice(lt * LANES, (lt + 1) * LANES)
                sh_ref[slab(c, branch, lt), pad - 1:pad, :] = ah[HALO_ROWS - 1:HALO_ROWS, lanes] * has_prev
                sh_ref[slab(c, branch, lt), pad:pad + tm, :] = a[:, lanes]
                sh_ref[slab(c, branch, lt), pad + tm:pad + tm + 1, :] = ah[HALO_ROWS:HALO_ROWS + 1, lanes] * has_next

    def conv(c, branch, cw_ref, cb_ref):
        outs = []
        for lt in range(lane_tiles):
            s = slab(c, branch, lt)
            cols = slice(c * sub + lt * LANES, c * sub + (lt + 1) * LANES)
            cw = cw_ref[:, cols]
            outs.append(cw[0:1, :] * sh_ref[s, pad - 1:pad - 1 + tm, :] + cw[1:2, :] * sh_ref[s, pad:pad + tm, :]
                        + cw[2:3, :] * sh_ref[s, pad + 1:pad + 1 + tm, :] + cb_ref[:, cols])
        return jnp.concatenate(outs, axis=1)

    def activate(c):
        gate = conv(c, 0, cwg_ref, cbg_ref)
        up = conv(c, 1, cwu_ref, cbu_ref)
        return (gate * up * (1.0 + jnp.tanh(gate * (c0 + c1 * (gate * gate))))).astype(BF16)

    nsub = wg_ref.shape[1] // sub
    for c in range(min(2, nsub)):
        project(c)
    for c in range(nsub):
        if c + 2 < nsub:
            project(c + 2)
        acc_ref[...] += _dot(activate(c), wd_ref[c * sub:(c + 1) * sub, :])

    @pl.when(j == nj - 1)
    def _():
        o_ref[...] = h_ref[...] + _rms(acc_ref[...], g_ref[...])


def _conv_ffn(u, h, w_up, conv_w, conv_b, w_down, g_post, seq):
    n, d = h.shape
    dff = w_down.shape[0]
    tm = min(FFN_TOKEN_TILE, seq)
    tf = FFN_COL_TILE
    nf = dff // tf
    hb = tm // HALO_ROWS
    nhalo = n // HALO_ROWS
    row = lambda i, j: (i, 0)
    return pl.pallas_call(
        functools.partial(_ffn_kernel, tiles_per_seq=seq // tm, sub=FFN_SUB_TILE),
        grid=(n // tm, nf),
        in_specs=[pl.BlockSpec((tm, d), row),
                  pl.BlockSpec((HALO_ROWS, d), lambda i, j: (jnp.maximum(i * hb - 1, 0), 0)),
                  pl.BlockSpec((HALO_ROWS, d), lambda i, j: (jnp.minimum((i + 1) * hb, nhalo - 1), 0)),
                  pl.BlockSpec((d, tf), lambda i, j: (0, j)),
                  pl.BlockSpec((d, tf), lambda i, j: (0, nf + j)),
                  pl.BlockSpec((CONV_WIDTH, tf), lambda i, j: (0, j)),
                  pl.BlockSpec((CONV_WIDTH, tf), lambda i, j: (0, nf + j)),
                  pl.BlockSpec((1, tf), lambda i, j: (0, j)),
                  pl.BlockSpec((1, tf), lambda i, j: (0, nf + j)),
                  pl.BlockSpec((tf, d), lambda i, j: (j, 0)),
                  pl.BlockSpec((tm, d), row),
                  pl.BlockSpec((1, d), lambda i, j: (0, 0))],
        out_specs=pl.BlockSpec((tm, d), row),
        out_shape=jax.ShapeDtypeStruct((n, d), F32),
        scratch_shapes=[pltpu.VMEM((tm, d), F32),
                        pltpu.VMEM((FFN_LOOKAHEAD_SETS * 2 * FFN_SUB_TILE // LANES, tm + 2 * SUBLANES, LANES), F32)],
        compiler_params=_cparams(("parallel", "arbitrary")),
        name="conv_ffn",
    )(u, u, u, w_up, w_up, conv_w, conv_w, conv_b, conv_b, w_down, h, g_post)


def _ple_kernel(h_ref, p_ref, wp_ref, wg_ref, g_ref, o_ref):
    h = h_ref[...]
    e = _rms(_dot(p_ref[...].astype(BF16), wp_ref[...]), g_ref[...])
    gate = _sigmoid(_dot(h.astype(BF16), wg_ref[...]))
    o_ref[...] = h + e * gate


def _ple(h, p, w_ple, w_gate, g):
    n, d = h.shape
    tm = WIDE_TOKEN_TILE
    row = lambda i: (i, 0)
    fixed = lambda i: (0, 0)
    return pl.pallas_call(
        _ple_kernel,
        grid=(n // tm,),
        in_specs=[pl.BlockSpec((tm, d), row), pl.BlockSpec((tm, p.shape[1]), row),
                  pl.BlockSpec(w_ple.shape, fixed), pl.BlockSpec(w_gate.shape, fixed), pl.BlockSpec((1, d), fixed)],
        out_specs=pl.BlockSpec((tm, d), row),
        out_shape=jax.ShapeDtypeStruct((n, d), F32),
        compiler_params=_cparams(("parallel",)),
        name="ple",
    )(h, p, w_ple, w_gate, g)


def kernel(x, p, rel_bias, g_pre_mix, w_in, lambda_q1, lambda_k1, lambda_q2, lambda_k2, g_diff, lb_param, g_hgrn, w_out, g_post_mix, g_pre_ffn, w_up, conv_w, conv_b, w_down, g_post_ffn, w_ple, g_ple, w_ple_gate):
    b, t, d = x.shape
    depth = w_in.shape[0]
    wd = d // 2
    wh = d - wd
    nh_diff = wd // HEAD_DIM
    nh_hgrn = wh // HEAD_DIM
    assert w_in.shape[2] == 3 * wd + 5 * wh
    assert t % ATT_BLOCK == 0 and t % HGRN_CHUNK == 0 and (b * t) % WIDE_TOKEN_TILE == 0

    rb = rel_bias.astype(F32)
    bias = _rel_bias_blocks(rb, ATT_BLOCK)
    h = x
    for l in range(depth):
        vec = lambda a: a[l].reshape(1, -1).astype(F32)
        hid = h
        qt, k, vt, hq, hi, hit, hf, hg = _in_proj(hid, vec(g_pre_mix), w_in[l].astype(BF16), wd, wh)
        lam_init = 0.8 - 0.6 * math.exp(-0.3 * l)
        a = _diff_attn(rb, qt, k, vt, bias, vec(lambda_q1), vec(lambda_k1), vec(lambda_q2), vec(lambda_k2),
                       vec(g_diff), nh_diff, lam_init)
        lbp = lb_param.astype(F32)
        o_fwd = _hgrn_dir(hq, hi, hit, hf, lbp, nh_hgrn, False, l)
        o = _hgrn_dir(hq, hi, hit, hf, lbp, nh_hgrn, True, l, extra=(o_fwd, hg, vec(g_hgrn)))
        n = b * t
        h1, u2 = _out_proj(a.reshape(n, wd), o.reshape(n, wh), hid.reshape(n, d), w_out[l].astype(BF16),
                           vec(g_post_mix), vec(g_pre_ffn))
        h2 = _conv_ffn(u2, h1, w_up[l].astype(BF16), conv_w[l].astype(F32), conv_b[l].reshape(1, -1).astype(F32),
                       (0.5 * w_down[l]).astype(BF16), vec(g_post_ffn), t)
        h3 = _ple(h2, p[l].reshape(n, -1), w_ple[l].astype(BF16), w_ple_gate[l].astype(BF16), vec(g_ple))
        h = h3.reshape(b, t, d)
    return h
```

```python
import functools
import math

import jax
import jax.numpy as jnp
from jax import lax
from jax.experimental import pallas as pl
from jax.experimental.pallas import tpu as pltpu

F32 = jnp.float32
BF16 = jnp.bfloat16

EPS = 1e-6
SUBLANES = 8
LANES = 128
DIFF_QK_DIM = 64
HEAD_DIM = 128
REL_BUCKETS = 32
REL_MAX_DIST = 128
CONV_WIDTH = 3

ATT_BLOCK = 512
ATT_STRIP = 512
ATT_QBLOCKS = 2
NEAR_BLOCKS = 3
ONES_ROWS = 16
LOG2E = math.log2(math.e)
TOP16 = -65536
HGRN_CHUNK = 128
HGRN_BASE = 16
HGRN_STEP = 1024
HGRN_HEADS = 4
TOKEN_TILE = 512
WIDE_TOKEN_TILE = 1024
FFN_TOKEN_TILE = 1024
FFN_COL_TILE = 1024
FFN_SUB_TILE = 256
FFN_LOOKAHEAD_SETS = 3
HALO_ROWS = 16
VMEM_LIMIT = 56 * 1024 * 1024


def _cparams(sem, flags=None):
    return pltpu.CompilerParams(dimension_semantics=sem, vmem_limit_bytes=VMEM_LIMIT, flags=flags)


def _rms(x, g):
    return x * lax.rsqrt(jnp.mean(x * x, axis=-1, keepdims=True) + EPS) * g


def _sigmoid(x):
    return 1.0 / (1.0 + jnp.exp(-x))


def _dot(a, b):
    return jnp.dot(a, b, preferred_element_type=F32)


def _dot_nt(a, b):
    return lax.dot_general(a, b, (((1,), (1,)), ((), ())), preferred_element_type=F32)


def _log_bucket_thresholds():
    half = REL_BUCKETS // 2
    max_exact = half // 2
    nlog = half - max_exact
    thr = []
    for k in range(1, nlog):
        n = max_exact
        while (n ** nlog) * (max_exact ** k) < (REL_MAX_DIST ** k) * (max_exact ** nlog):
            n += 1
        thr.append(n)
    return max_exact, half, thr


def _bias_kernel(rb_ref, o_ref, *, blk):
    h = pl.program_id(0)
    d = pl.program_id(1) - 2
    max_exact, half, thr = _log_bucket_thresholds()
    width = 2 * blk
    z = lax.broadcasted_iota(jnp.int32, (SUBLANES, width), 1)
    z = jnp.where(z < blk, z, z - width)
    rel = d * blk - z
    n = jnp.abs(rel)
    large = jnp.full(rel.shape, max_exact, jnp.int32)
    for t in thr:
        large = large + (n >= t).astype(jnp.int32)
    bucket = jnp.where(rel > 0, half, 0) + jnp.where(n < max_exact, n, large)
    bias = jnp.zeros(rel.shape, F32)
    for b in range(REL_BUCKETS):
        bias = jnp.where(bucket == b, rb_ref[b, h], bias)
    rows = jnp.broadcast_to(bias[0:1, :] * LOG2E, (blk, width))
    o_ref[0, 0] = pltpu.roll(rows, 0, 1, stride=1, stride_axis=0)[:, :blk]


def _rel_bias_blocks(rel_bias, blk):
    assert blk >= REL_MAX_DIST
    nh = rel_bias.shape[1]
    return pl.pallas_call(
        functools.partial(_bias_kernel, blk=blk),
        grid=(nh, 5),
        in_specs=[pl.BlockSpec(memory_space=pltpu.SMEM)],
        out_specs=pl.BlockSpec((1, 1, blk, blk), lambda h, d: (h, d, 0, 0)),
        out_shape=jax.ShapeDtypeStruct((nh, 5, blk, blk), F32),
        compiler_params=_cparams(("parallel", "parallel")),
        name="rel_bias_blocks",
    )(rel_bias)


def _in_proj_kernel(x_ref, g_ref, w_ref, qt_ref, k_ref, vt_ref, hq_ref, hi_ref, hit_ref, hf_ref, hg_ref,
                    *, wd, wh):
    u = _rms(x_ref[0], g_ref[...]).astype(BF16)

    def proj(a, b):
        return _dot(u, w_ref[:, a:b])

    qt_ref[0] = (proj(0, wd) * (DIFF_QK_DIM ** -0.5 * LOG2E)).T.astype(BF16)
    kp = proj(wd, 2 * wd).astype(BF16)
    for h in range(wd // HEAD_DIM):
        k_ref[0, h] = kp[:, h * HEAD_DIM:(h + 1) * HEAD_DIM]
    vt_ref[0] = proj(2 * wd, 3 * wd).T.astype(BF16)
    o = 3 * wd
    qh = proj(o, o + wh)
    hq_ref[0] = (qh * _sigmoid(qh)).astype(BF16)
    ih = proj(o + wh, o + 2 * wh)
    hi_ref[0] = ih.astype(BF16)
    hit_ref[0] = ih.T.astype(BF16)
    hf_ref[0] = proj(o + 2 * wh, o + 4 * wh)
    gh = proj(o + 4 * wh, o + 5 * wh)
    hg_ref[0] = gh * _sigmoid(gh)


def _in_proj(x, g, w, wd, wh):
    b, t, d = x.shape
    tm = TOKEN_TILE
    ncol = w.shape[1]
    row = lambda bi, ti: (bi, ti, 0)
    col = lambda bi, ti: (bi, 0, ti)
    return pl.pallas_call(
        functools.partial(_in_proj_kernel, wd=wd, wh=wh),
        grid=(b, t // tm),
        in_specs=[pl.BlockSpec((1, tm, d), row),
                  pl.BlockSpec((1, d), lambda bi, ti: (0, 0)),
                  pl.BlockSpec((d, ncol), lambda bi, ti: (0, 0))],
        out_specs=[pl.BlockSpec((1, wd, tm), col),
                   pl.BlockSpec((1, wd // HEAD_DIM, tm, HEAD_DIM), lambda bi, ti: (bi, 0, ti, 0)),
                   pl.BlockSpec((1, wd, tm), col),
                   pl.BlockSpec((1, tm, wh), row),
                   pl.BlockSpec((1, tm, wh), row),
                   pl.BlockSpec((1, wh, tm), col),
                   pl.BlockSpec((1, tm, 2 * wh), row),
                   pl.BlockSpec((1, tm, wh), row)],
        out_shape=[jax.ShapeDtypeStruct((b, wd, t), BF16),
                   jax.ShapeDtypeStruct((b, wd // HEAD_DIM, t, HEAD_DIM), BF16),
                   jax.ShapeDtypeStruct((b, wd, t), BF16),
                   jax.ShapeDtypeStruct((b, t, wh), BF16),
                   jax.ShapeDtypeStruct((b, t, wh), BF16),
                   jax.ShapeDtypeStruct((b, wh, t), BF16),
                   jax.ShapeDtypeStruct((b, t, 2 * wh), F32),
                   jax.ShapeDtypeStruct((b, t, wh), F32)],
        compiler_params=_cparams(("parallel", "parallel")),
        name="in_proj",
    )(x, g, w)


def _attn_kernel(rb_ref, qt_ref, k_ref, vt_ref, bias_ref, lq1_ref, lk1_ref, lq2_ref, lk2_ref, gd_ref,
                 o_ref, qq_ref, m_ref, mx_ref, st_ref, acc_ref, s_ref, p_ref, *, blk, strip, seq, nq, lam_init):
    h = pl.program_id(0)
    nblk = seq // blk
    nstrip = blk // strip
    ones = jnp.ones((ONES_ROWS, blk), BF16)
    half = REL_BUCKETS // 2
    far_left = rb_ref[half - 1, h] * LOG2E
    far_right = rb_ref[2 * half - 1, h] * LOG2E
    lam = (jnp.exp(jnp.sum(lq1_ref[...] * lk1_ref[...])) - jnp.exp(jnp.sum(lq2_ref[...] * lk2_ref[...]))
           + lam_init)

    for g in range(nq):
        qt = qt_ref[0, :, g * blk:(g + 1) * blk]
        sub = lax.broadcasted_iota(jnp.int32, qt.shape, 0)
        zero = jnp.zeros_like(qt)
        qq_ref[g, :, 0:blk] = jnp.where(sub < DIFF_QK_DIM, qt, zero)
        qq_ref[g, :, blk:2 * blk] = jnp.where(sub >= DIFF_QK_DIM, qt, zero)

    def query_block(j):
        return pl.program_id(2) * nq + j // nblk

    def key_block(j):
        kb = query_block(j) - 1 + j % nblk
        kb = jnp.where(kb < 0, kb + nblk, kb)
        return jnp.where(kb >= nblk, kb - nblk, kb)

    def near(j):
        return j % nblk < NEAR_BLOCKS

    def scores(j):
        k0 = pl.multiple_of(key_block(j) * blk, blk)
        mx8 = jnp.full((8, 2 * blk), -jnp.inf, F32)
        for r in range(nstrip):
            s = _dot(k_ref[0, 0, pl.ds(k0 + r * strip, strip), :], qq_ref[j // nblk])
            s_ref[j % 3, r * strip:(r + 1) * strip, :] = s
            if not near(j):
                for i in range(strip // 8):
                    mx8 = jnp.maximum(mx8, s[i * 8:(i + 1) * 8])
        if not near(j):
            mx_ref[j % 4] = mx8

    def biased(j, r):
        rows = slice(r * strip, (r + 1) * strip)
        s = s_ref[j % 3, rows, :]
        if near(j):
            b = bias_ref[0, jnp.clip(key_block(j) - query_block(j), -2, 2) + 2, rows, :]
            s = s + jnp.concatenate([b, b], axis=1)
        return s

    def stats(j):
        g = j // nblk
        if near(j):
            mx8 = jnp.full((8, 2 * blk), -jnp.inf, F32)
            for r in range(nstrip):
                s = biased(j, r)
                for i in range(strip // 8):
                    mx8 = jnp.maximum(mx8, s[i * 8:(i + 1) * 8])
        else:
            mx8 = mx_ref[j % 4]
        shift = 0.0 if near(j) else jnp.where(key_block(j) < query_block(j), far_left, far_right)
        m_blk = jnp.max(mx8, axis=0, keepdims=True) + shift
        if j % nblk == 0:
            m_new = m_blk
        else:
            m_prev = m_ref[g]
            m_new = jnp.maximum(m_prev, m_blk)
            st_ref[j % 4, 0:1, :] = jnp.exp2(m_prev - m_new)
        m_ref[g] = m_new
        st_ref[j % 4, 1:2, :] = m_new - shift

    def probs(j):
        off = st_ref[j % 4, 1:2, :]
        for r in range(nstrip):
            p_ref[j % 2, r * strip:(r + 1) * strip, :] = jnp.exp2(biased(j, r) - off).astype(BF16)

    def values(j):
        g = j // nblk
        k0 = pl.multiple_of(key_block(j) * blk, blk)
        vta = jnp.concatenate([vt_ref[0, :, pl.ds(k0, blk)], ones], axis=0)
        pv = _dot(vta, p_ref[j % 2])
        acc_ref[g] = pv if j % nblk == 0 else st_ref[j % 4, 0:1, :] * acc_ref[g] + pv
        if j % nblk == nblk - 1:
            acc = acc_ref[g]
            ot = acc[:HEAD_DIM] / acc[HEAD_DIM:HEAD_DIM + 1]
            o = (ot[:, :blk] - lam * ot[:, blk:]).T
            o_ref[0, g * blk:(g + 1) * blk, :] = (_rms(o, gd_ref[...]) * (1.0 - lam_init)).astype(BF16)

    njobs = nq * nblk
    for stage in range(njobs + 3):
        for step, lag in ((values, 3), (scores, 0), (stats, 1), (probs, 2)):
            if 0 <= stage - lag < njobs:
                step(stage - lag)


def _diff_attn(rel_bias, qt, k, vt, bias, lq1, lk1, lq2, lk2, g_diff, nh, lam_init):
    b, _, t, _ = k.shape
    blk = ATT_BLOCK
    nq = ATT_QBLOCKS
    assert t // blk >= NEAR_BLOCKS and (t // blk) % nq == 0
    small = lambda h, bi, qi: (0, 0)
    return pl.pallas_call(
        functools.partial(_attn_kernel, blk=blk, strip=ATT_STRIP, seq=t, nq=nq, lam_init=lam_init),
        grid=(nh, b, t // (nq * blk)),
        in_specs=[pl.BlockSpec(memory_space=pltpu.SMEM),
                  pl.BlockSpec((1, HEAD_DIM, nq * blk), lambda h, bi, qi: (bi, h, qi)),
                  pl.BlockSpec((1, 1, t, HEAD_DIM), lambda h, bi, qi: (bi, h, 0, 0)),
                  pl.BlockSpec((1, HEAD_DIM, t), lambda h, bi, qi: (bi, h, 0)),
                  pl.BlockSpec((1, 5, blk, blk), lambda h, bi, qi: (h, 0, 0, 0)),
                  pl.BlockSpec((1, DIFF_QK_DIM), small), pl.BlockSpec((1, DIFF_QK_DIM), small),
                  pl.BlockSpec((1, DIFF_QK_DIM), small), pl.BlockSpec((1, DIFF_QK_DIM), small),
                  pl.BlockSpec((1, HEAD_DIM), small)],
        out_specs=pl.BlockSpec((1, nq * blk, HEAD_DIM), lambda h, bi, qi: (bi, qi, h)),
        out_shape=jax.ShapeDtypeStruct((b, t, nh * HEAD_DIM), BF16),
        scratch_shapes=[pltpu.VMEM((nq, HEAD_DIM, 2 * blk), BF16),
                        pltpu.VMEM((nq, 1, 2 * blk), F32),
                        pltpu.VMEM((4, 8, 2 * blk), F32),
                        pltpu.VMEM((4, 2, 2 * blk), F32),
                        pltpu.VMEM((nq, HEAD_DIM + ONES_ROWS, 2 * blk), F32),
                        pltpu.VMEM((3, blk, 2 * blk), F32),
                        pltpu.VMEM((2, blk, 2 * blk), BF16)],
        compiler_params=_cparams(("parallel", "parallel", "arbitrary")),
        name="diff_attn",
    )(rel_bias, qt, k, vt, bias, lq1, lk1, lq2, lk2, g_diff)


def _hgrn_masks(rev):
    c = HGRN_CHUNK
    row = lax.broadcasted_iota(jnp.int32, (c, c), 0)
    col = lax.broadcasted_iota(jnp.int32, (c, c), 1)
    tri = (col >= row) if rev else (col <= row)
    masks = []
    b = HGRN_BASE
    while b < c:
        masks.append((b, ((row ^ col) >> int(math.log2(b))) == 1))
        b *= 2
    sh = int(math.log2(HGRN_BASE))
    diag = ((row >> sh) == (col >> sh)) & tri
    return tri.astype(BF16), masks, diag


def _row_refs(cum, idxs, rows_each):
    return jnp.concatenate(
        [jnp.broadcast_to(cum[i:i + 1, :], (rows_each, cum.shape[1])) for i in idxs], axis=0)


def _hgrn_chunks(q, kk, logf, vs, vts, st, tri, masks, diag, rev):
    c = HGRN_CHUNK
    n = len(vs)
    d = q.shape[1] // n
    part = lambda x, i: x[:, i * d:(i + 1) * d]
    top = lax.bitcast_convert_type(lax.bitcast_convert_type(logf, jnp.int32) & TOP16, F32)
    cum = _dot(tri, top.astype(BF16)) + _dot(tri, (logf - top).astype(BF16))
    last = cum[0:1, :] if rev else cum[c - 1:c, :]
    rowi = lax.broadcasted_iota(jnp.int32, q.shape, 0)

    a = [jnp.zeros((c, c), F32)] * n
    for b, siblings in masks:
        ngroups = c // (2 * b)
        ref = _row_refs(cum, [g * 2 * b + (b if rev else b - 1) for g in range(ngroups)], 2 * b)
        first_half = (rowi & (2 * b - 1)) < b
        is_q = first_half if rev else jnp.logical_not(first_half)
        qt = jnp.where(is_q, q * jnp.exp(jnp.minimum(cum - ref, 0.0)), 0.0).astype(BF16)
        kt = jnp.where(is_q, 0.0, kk * jnp.exp(jnp.minimum(ref - cum, 0.0))).astype(BF16)
        a = [jnp.where(siblings, _dot_nt(part(qt, i), part(kt, i)), a[i]) for i in range(n)]
    nb = c // HGRN_BASE
    ref = _row_refs(cum, [g * HGRN_BASE + HGRN_BASE // 2 for g in range(nb)], HGRN_BASE)
    qd = (q * jnp.exp(cum - ref)).astype(BF16)
    kd = (kk * jnp.exp(ref - cum)).astype(BF16)
    a = [jnp.where(diag, _dot_nt(part(qd, i), part(kd, i)), a[i]) for i in range(n)]

    qe = (q * jnp.exp(cum)).astype(BF16)
    kl = (kk * jnp.exp(last - cum)).astype(BF16)
    decay = jnp.exp(last)
    intra = [_dot(a[i].astype(BF16), vs[i]) for i in range(n)]
    update = [_dot(vts[i], part(kl, i)) for i in range(n)]
    outs = [None] * n
    for i in (range(n - 1, -1, -1) if rev else range(n)):
        outs[i] = intra[i] + _dot_nt(part(qe, i), st.astype(BF16))
        st = st * part(decay, i) + update[i]
    return outs, st


def _hgrn_kernel(*refs, rev, nchunk, level):
    if rev:
        q_ref, v_ref, vt_ref, f_ref, lb_ref, of_ref, g_ref, gn_ref, o_ref, st_ref = refs
    else:
        q_ref, v_ref, vt_ref, f_ref, lb_ref, o_ref, st_ref = refs

    @pl.when(pl.program_id(2) == 0)
    def _():
        st_ref[...] = jnp.zeros(st_ref.shape, F32)

    tri, masks, diag = _hgrn_masks(rev)
    c = HGRN_CHUNK
    d = HEAD_DIM
    chunks = [slice(ci * c, (ci + 1) * c) for ci in range(nchunk)]
    for hh in range(q_ref.shape[2] // d):
        head = slice(hh * d, (hh + 1) * d)
        lp = lb_ref[0, :, head]
        e = jnp.exp(lp - jnp.max(lp, axis=0, keepdims=True))
        lb = jnp.sum(e[0:level + 1], axis=0, keepdims=True) / jnp.sum(e, axis=0, keepdims=True)

        wide = lambda ref: jnp.concatenate([ref[0, rows, head] for rows in chunks], axis=1)
        lbw = jnp.concatenate([lb] * nchunk, axis=1)
        f = lbw + (1.0 - lbw) * _sigmoid(wide(f_ref))
        outs, st = _hgrn_chunks(wide(q_ref).astype(F32), 1.0 - f, jnp.log(f),
                                [v_ref[0, rows, head] for rows in chunks],
                                [vt_ref[0, head, rows] for rows in chunks],
                                st_ref[hh], tri, masks, diag, rev)
        st_ref[hh] = st
        for rows, out in zip(chunks, outs):
            if rev:
                o = out + of_ref[0, rows, head]
                o_ref[0, rows, head] = (_rms(o, gn_ref[...]) * g_ref[0, rows, head]).astype(BF16)
            else:
                o_ref[0, rows, head] = out


def _hgrn_dir(hq, hi, hit, hf, lb_param, nh, rev, level, extra=()):
    b, t, _ = hq.shape
    step = min(HGRN_STEP, t)
    nstep = t // step
    hp = HGRN_HEADS
    d = hp * HEAD_DIM
    assert nh % hp == 0
    tpos = (lambda n: nstep - 1 - n) if rev else (lambda n: n)
    row = lambda bi, h, n: (bi, tpos(n), h)
    in_specs = [pl.BlockSpec((1, step, d), row),
                pl.BlockSpec((1, step, d), row),
                pl.BlockSpec((1, d, step), lambda bi, h, n: (bi, h, tpos(n))),
                pl.BlockSpec((1, step, d), lambda bi, h, n: (bi, tpos(n), (nh // hp if rev else 0) + h)),
                pl.BlockSpec((1, lb_param.shape[1], d), lambda bi, h, n: (1 if rev else 0, 0, h))]
    if rev:
        in_specs += [pl.BlockSpec((1, step, d), row), pl.BlockSpec((1, step, d), row),
                     pl.BlockSpec((1, HEAD_DIM), lambda bi, h, n: (0, 0))]
    return pl.pallas_call(
        functools.partial(_hgrn_kernel, rev=rev, nchunk=step // HGRN_CHUNK, level=level),
        grid=(b, nh // hp, nstep),
        in_specs=in_specs,
        out_specs=pl.BlockSpec((1, step, d), row),
        out_shape=jax.ShapeDtypeStruct((b, t, nh * HEAD_DIM), BF16 if rev else F32),
        scratch_shapes=[pltpu.VMEM((hp, HEAD_DIM, HEAD_DIM), F32)],
        compiler_params=_cparams(("parallel", "parallel", "arbitrary")),
        name="hgrn_bwd" if rev else "hgrn_fwd",
    )(hq, hi, hit, hf, lb_param, *extra)


def _out_proj_kernel(a_ref, o_ref, x_ref, w_ref, gp_ref, gf_ref, h_ref, u_ref, *, wd):
    m = _dot(a_ref[...], w_ref[0:wd, :]) + _dot(o_ref[...], w_ref[wd:, :])
    h = x_ref[...] + _rms(m, gp_ref[...])
    h_ref[...] = h
    u_ref[...] = _rms(h, gf_ref[...]).astype(BF16)


def _out_proj(a, o, x, w, g_post, g_ffn):
    n, d = x.shape
    wd = a.shape[1]
    tm = WIDE_TOKEN_TILE
    row = lambda i: (i, 0)
    fixed = lambda i: (0, 0)
    return pl.pallas_call(
        functools.partial(_out_proj_kernel, wd=wd),
        grid=(n // tm,),
        in_specs=[pl.BlockSpec((tm, wd), row), pl.BlockSpec((tm, o.shape[1]), row), pl.BlockSpec((tm, d), row),
                  pl.BlockSpec(w.shape, fixed), pl.BlockSpec((1, d), fixed), pl.BlockSpec((1, d), fixed)],
        out_specs=[pl.BlockSpec((tm, d), row), pl.BlockSpec((tm, d), row)],
        out_shape=[jax.ShapeDtypeStruct((n, d), F32), jax.ShapeDtypeStruct((n, d), BF16)],
        compiler_params=_cparams(("parallel",)),
        name="out_proj",
    )(a, o, x, w, g_post, g_ffn)


def _ffn_kernel(u_ref, up_ref, un_ref, wg_ref, wu_ref, cwg_ref, cwu_ref, cbg_ref, cbu_ref, wd_ref,
                h_ref, g_ref, o_ref, acc_ref, sh_ref, *, tiles_per_seq, sub):
    i = pl.program_id(0)
    j = pl.program_id(1)
    nj = pl.num_programs(1)
    tm = u_ref.shape[0]

    @pl.when(j == 0)
    def _():
        acc_ref[...] = jnp.zeros(acc_ref.shape, F32)

    u = u_ref[...]
    halo = jnp.concatenate([up_ref[...], un_ref[...]], axis=0)
    ti = i % tiles_per_seq
    has_prev = jnp.where(ti > 0, 1.0, 0.0)
    has_next = jnp.where(ti < tiles_per_seq - 1, 1.0, 0.0)
    c0 = math.sqrt(2.0 / math.pi)
    c1 = c0 * 0.044715
    lane_tiles = sub // LANES
    pad = SUBLANES

    def slab(c, branch, lt):
        return ((c % FFN_LOOKAHEAD_SETS) * 2 + branch) * lane_tiles + lt

    def project(c):
        cols = slice(c * sub, (c + 1) * sub)
        for branch, w_ref in enumerate((wg_ref, wu_ref)):
            w = w_ref[:, cols]
            a = _dot(u, w)
            ah = _dot(halo, w)
            for lt in range(lane_tiles):
                lanes = slice(lt * LANES, (lt + 1) * LANES)
                sh_ref[slab(c, branch, lt), pad - 1:pad, :] = ah[HALO_ROWS - 1:HALO_ROWS, lanes] * has_prev
                sh_ref[slab(c, branch, lt), pad:pad + tm, :] = a[:, lanes]
                sh_ref[slab(c, branch, lt), pad + tm:pad + tm + 1, :] = ah[HALO_ROWS:HALO_ROWS + 1, lanes] * has_next

    def conv(c, branch, cw_ref, cb_ref):
        outs = []
        for lt in range(lane_tiles):
            s = slab(c, branch, lt)
            cols = slice(c * sub + lt * LANES, c * sub + (lt + 1) * LANES)
            cw = cw_ref[:, cols]
            outs.append(cw[0:1, :] * sh_ref[s, pad - 1:pad - 1 + tm, :] + cw[1:2, :] * sh_ref[s, pad:pad + tm, :]
                        + cw[2:3, :] * sh_ref[s, pad + 1:pad + 1 + tm, :] + cb_ref[:, cols])
        return jnp.concatenate(outs, axis=1)

    def activate(c):
        gate = conv(c, 0, cwg_ref, cbg_ref)
        up = conv(c, 1, cwu_ref, cbu_ref)
        return (gate * up * (1.0 + jnp.tanh(gate * (c0 + c1 * (gate * gate))))).astype(BF16)

    nsub = wg_ref.shape[1] // sub
    for c in range(min(2, nsub)):
        project(c)
    for c in range(nsub):
        if c + 2 < nsub:
            project(c + 2)
        acc_ref[...] += _dot(activate(c), wd_ref[c * sub:(c + 1) * sub, :])

    @pl.when(j == nj - 1)
    def _():
        o_ref[...] = h_ref[...] + _rms(acc_ref[...], g_ref[...])


def _conv_ffn(u, h, w_up, conv_w, conv_b, w_down, g_post, seq):
    n, d = h.shape
    dff = w_down.shape[0]
    tm = min(FFN_TOKEN_TILE, seq)
    tf = FFN_COL_TILE
    nf = dff // tf
    hb = tm // HALO_ROWS
    nhalo = n // HALO_ROWS
    row = lambda i, j: (i, 0)
    return pl.pallas_call(
        functools.partial(_ffn_kernel, tiles_per_seq=seq // tm, sub=FFN_SUB_TILE),
        grid=(n // tm, nf),
        in_specs=[pl.BlockSpec((tm, d), row),
                  pl.BlockSpec((HALO_ROWS, d), lambda i, j: (jnp.maximum(i * hb - 1, 0), 0)),
                  pl.BlockSpec((HALO_ROWS, d), lambda i, j: (jnp.minimum((i + 1) * hb, nhalo - 1), 0)),
                  pl.BlockSpec((d, tf), lambda i, j: (0, j)),
                  pl.BlockSpec((d, tf), lambda i, j: (0, nf + j)),
                  pl.BlockSpec((CONV_WIDTH, tf), lambda i, j: (0, j)),
                  pl.BlockSpec((CONV_WIDTH, tf), lambda i, j: (0, nf + j)),
                  pl.BlockSpec((1, tf), lambda i, j: (0, j)),
                  pl.BlockSpec((1, tf), lambda i, j: (0, nf + j)),
                  pl.BlockSpec((tf, d), lambda i, j: (j, 0)),
                  pl.BlockSpec((tm, d), row),
                  pl.BlockSpec((1, d), lambda i, j: (0, 0))],
        out_specs=pl.BlockSpec((tm, d), row),
        out_shape=jax.ShapeDtypeStruct((n, d), F32),
        scratch_shapes=[pltpu.VMEM((tm, d), F32),
                        pltpu.VMEM((FFN_LOOKAHEAD_SETS * 2 * FFN_SUB_TILE // LANES, tm + 2 * SUBLANES, LANES), F32)],
        compiler_params=_cparams(("parallel", "arbitrary")),
        name="conv_ffn",
    )(u, u, u, w_up, w_up, conv_w, conv_w, conv_b, conv_b, w_down, h, g_post)


def _ple_kernel(h_ref, p_ref, wp_ref, wg_ref, g_ref, o_ref):
    h = h_ref[...]
    e = _rms(_dot(p_ref[...].astype(BF16), wp_ref[...]), g_ref[...])
    gate = _sigmoid(_dot(h.astype(BF16), wg_ref[...]))
    o_ref[...] = h + e * gate


def _ple(h, p, w_ple, w_gate, g):
    n, d = h.shape
    tm = WIDE_TOKEN_TILE
    row = lambda i: (i, 0)
    fixed = lambda i: (0, 0)
    return pl.pallas_call(
        _ple_kernel,
        grid=(n // tm,),
        in_specs=[pl.BlockSpec((tm, d), row), pl.BlockSpec((tm, p.shape[1]), row),
                  pl.BlockSpec(w_ple.shape, fixed), pl.BlockSpec(w_gate.shape, fixed), pl.BlockSpec((1, d), fixed)],
        out_specs=pl.BlockSpec((tm, d), row),
        out_shape=jax.ShapeDtypeStruct((n, d), F32),
        compiler_params=_cparams(("parallel",)),
        name="ple",
    )(h, p, w_ple, w_gate, g)


def kernel(x, p, rel_bias, g_pre_mix, w_in, lambda_q1, lambda_k1, lambda_q2, lambda_k2, g_diff, lb_param, g_hgrn, w_out, g_post_mix, g_pre_ffn, w_up, conv_w, conv_b, w_down, g_post_ffn, w_ple, g_ple, w_ple_gate):
    b, t, d = x.shape
    depth = w_in.shape[0]
    wd = d // 2
    wh = d - wd
    nh_diff = wd // HEAD_DIM
    nh_hgrn = wh // HEAD_DIM
    assert w_in.shape[2] == 3 * wd + 5 * wh
    assert t % ATT_BLOCK == 0 and t % HGRN_CHUNK == 0 and (b * t) % WIDE_TOKEN_TILE == 0

    rb = rel_bias.astype(F32)
    bias = _rel_bias_blocks(rb, ATT_BLOCK)
    h = x
    for l in range(depth):
        vec = lambda a: a[l].reshape(1, -1).astype(F32)
        hid = h
        qt, k, vt, hq, hi, hit, hf, hg = _in_proj(hid, vec(g_pre_mix), w_in[l].astype(BF16), wd, wh)
        lam_init = 0.8 - 0.6 * math.exp(-0.3 * l)
        a = _diff_attn(rb, qt, k, vt, bias, vec(lambda_q1), vec(lambda_k1), vec(lambda_q2), vec(lambda_k2),
                       vec(g_diff), nh_diff, lam_init)
        lbp = lb_param.astype(F32)
        o_fwd = _hgrn_dir(hq, hi, hit, hf, lbp, nh_hgrn, False, l)
        o = _hgrn_dir(hq, hi, hit, hf, lbp, nh_hgrn, True, l, extra=(o_fwd, hg, vec(g_hgrn)))
        n = b * t
        h1, u2 = _out_proj(a.reshape(n, wd), o.reshape(n, wh), hid.reshape(n, d), w_out[l].astype(BF16),
                           vec(g_post_mix), vec(g_pre_ffn))
        h2 = _conv_ffn(u2, h1, w_up[l].astype(BF16), conv_w[l].astype(F32), conv_b[l].reshape(1, -1).astype(F32),
                       (0.5 * w_down[l]).astype(BF16), vec(g_post_ffn), t)
        h3 = _ple(h2, p[l].reshape(n, -1), w_ple[l].astype(BF16), w_ple_gate[l].astype(BF16), vec(g_ple))
        h = h3.reshape(b, t, d)
    return h
```

```python
import functools
import math

import jax
import jax.numpy as jnp
from jax import lax
from jax.experimental import pallas as pl
from jax.experimental.pallas import tpu as pltpu

F32 = jnp.float32
BF16 = jnp.bfloat16

EPS = 1e-6
SUBLANES = 8
LANES = 128
DIFF_QK_DIM = 64
HEAD_DIM = 128
REL_BUCKETS = 32
REL_MAX_DIST = 128
CONV_WIDTH = 3

ATT_BLOCK = 512
ATT_STRIP = 512
ATT_QBLOCKS = 2
NEAR_BLOCKS = 3
ONES_ROWS = 16
LOG2E = math.log2(math.e)
TOP16 = -65536
HGRN_CHUNK = 128
HGRN_BASE = 16
HGRN_STEP = 1024
HGRN_HEADS = 4
TOKEN_TILE = 512
WIDE_TOKEN_TILE = 1024
FFN_TOKEN_TILE = 1024
FFN_COL_TILE = 1024
FFN_SUB_TILE = 256
FFN_LOOKAHEAD = 3
FFN_LOOKAHEAD_SETS = 4
HALO_ROWS = 16
VMEM_LIMIT = 56 * 1024 * 1024


def _cparams(sem, flags=None):
    return pltpu.CompilerParams(dimension_semantics=sem, vmem_limit_bytes=VMEM_LIMIT, flags=flags)


def _rms(x, g):
    return x * lax.rsqrt(jnp.mean(x * x, axis=-1, keepdims=True) + EPS) * g


def _sigmoid(x):
    return 1.0 / (1.0 + jnp.exp(-x))


def _dot(a, b):
    return jnp.dot(a, b, preferred_element_type=F32)


def _dot_nt(a, b):
    return lax.dot_general(a, b, (((1,), (1,)), ((), ())), preferred_element_type=F32)


def _log_bucket_thresholds():
    half = REL_BUCKETS // 2
    max_exact = half // 2
    nlog = half - max_exact
    thr = []
    for k in range(1, nlog):
        n = max_exact
        while (n ** nlog) * (max_exact ** k) < (REL_MAX_DIST ** k) * (max_exact ** nlog):
            n += 1
        thr.append(n)
    return max_exact, half, thr


def _bias_kernel(rb_ref, o_ref, *, blk):
    h = pl.program_id(0)
    d = pl.program_id(1) - 2
    max_exact, half, thr = _log_bucket_thresholds()
    width = 2 * blk
    z = lax.broadcasted_iota(jnp.int32, (SUBLANES, width), 1)
    z = jnp.where(z < blk, z, z - width)
    rel = d * blk - z
    n = jnp.abs(rel)
    large = jnp.full(rel.shape, max_exact, jnp.int32)
    for t in thr:
        large = large + (n >= t).astype(jnp.int32)
    bucket = jnp.where(rel > 0, half, 0) + jnp.where(n < max_exact, n, large)
    bias = jnp.zeros(rel.shape, F32)
    for b in range(REL_BUCKETS):
        bias = jnp.where(bucket == b, rb_ref[b, h], bias)
    rows = jnp.broadcast_to(bias[0:1, :] * LOG2E, (blk, width))
    o_ref[0, 0] = pltpu.roll(rows, 0, 1, stride=1, stride_axis=0)[:, :blk]


def _rel_bias_blocks(rel_bias, blk):
    assert blk >= REL_MAX_DIST
    nh = rel_bias.shape[1]
    return pl.pallas_call(
        functools.partial(_bias_kernel, blk=blk),
        grid=(nh, 5),
        in_specs=[pl.BlockSpec(memory_space=pltpu.SMEM)],
        out_specs=pl.BlockSpec((1, 1, blk, blk), lambda h, d: (h, d, 0, 0)),
        out_shape=jax.ShapeDtypeStruct((nh, 5, blk, blk), F32),
        compiler_params=_cparams(("parallel", "parallel")),
        name="rel_bias_blocks",
    )(rel_bias)


def _in_proj_kernel(x_ref, g_ref, w_ref, qt_ref, k_ref, vt_ref, hq_ref, hi_ref, hit_ref, hf_ref, hg_ref,
                    *, wd, wh):
    u = _rms(x_ref[0], g_ref[...]).astype(BF16)

    def proj(a, b):
        return _dot(u, w_ref[:, a:b])

    qt_ref[0] = (proj(0, wd) * (DIFF_QK_DIM ** -0.5 * LOG2E)).T.astype(BF16)
    kp = proj(wd, 2 * wd).astype(BF16)
    for h in range(wd // HEAD_DIM):
        k_ref[0, h] = kp[:, h * HEAD_DIM:(h + 1) * HEAD_DIM]
    vt_ref[0] = proj(2 * wd, 3 * wd).T.astype(BF16)
    o = 3 * wd
    qh = proj(o, o + wh)
    hq_ref[0] = (qh * _sigmoid(qh)).astype(BF16)
    ih = proj(o + wh, o + 2 * wh)
    hi_ref[0] = ih.astype(BF16)
    hit_ref[0] = ih.T.astype(BF16)
    hf_ref[0] = proj(o + 2 * wh, o + 4 * wh)
    gh = proj(o + 4 * wh, o + 5 * wh)
    hg_ref[0] = gh * _sigmoid(gh)


def _in_proj(x, g, w, wd, wh):
    b, t, d = x.shape
    tm = TOKEN_TILE
    ncol = w.shape[1]
    row = lambda bi, ti: (bi, ti, 0)
    col = lambda bi, ti: (bi, 0, ti)
    return pl.pallas_call(
        functools.partial(_in_proj_kernel, wd=wd, wh=wh),
        grid=(b, t // tm),
        in_specs=[pl.BlockSpec((1, tm, d), row),
                  pl.BlockSpec((1, d), lambda bi, ti: (0, 0)),
                  pl.BlockSpec((d, ncol), lambda bi, ti: (0, 0))],
        out_specs=[pl.BlockSpec((1, wd, tm), col),
                   pl.BlockSpec((1, wd // HEAD_DIM, tm, HEAD_DIM), lambda bi, ti: (bi, 0, ti, 0)),
                   pl.BlockSpec((1, wd, tm), col),
                   pl.BlockSpec((1, tm, wh), row),
                   pl.BlockSpec((1, tm, wh), row),
                   pl.BlockSpec((1, wh, tm), col),
                   pl.BlockSpec((1, tm, 2 * wh), row),
                   pl.BlockSpec((1, tm, wh), row)],
        out_shape=[jax.ShapeDtypeStruct((b, wd, t), BF16),
                   jax.ShapeDtypeStruct((b, wd // HEAD_DIM, t, HEAD_DIM), BF16),
                   jax.ShapeDtypeStruct((b, wd, t), BF16),
                   jax.ShapeDtypeStruct((b, t, wh), BF16),
                   jax.ShapeDtypeStruct((b, t, wh), BF16),
                   jax.ShapeDtypeStruct((b, wh, t), BF16),
                   jax.ShapeDtypeStruct((b, t, 2 * wh), F32),
                   jax.ShapeDtypeStruct((b, t, wh), F32)],
        compiler_params=_cparams(("parallel", "parallel")),
        name="in_proj",
    )(x, g, w)


def _attn_kernel(rb_ref, qt_ref, k_ref, vt_ref, bias_ref, lq1_ref, lk1_ref, lq2_ref, lk2_ref, gd_ref,
                 o_ref, qq_ref, m_ref, mx_ref, st_ref, acc_ref, s_ref, p_ref, *, blk, strip, seq, nq, lam_init):
    h = pl.program_id(0)
    nblk = seq // blk
    nstrip = blk // strip
    ones = jnp.ones((ONES_ROWS, blk), BF16)
    half = REL_BUCKETS // 2
    far_left = rb_ref[half - 1, h] * LOG2E
    far_right = rb_ref[2 * half - 1, h] * LOG2E
    lam = (jnp.exp(jnp.sum(lq1_ref[...] * lk1_ref[...])) - jnp.exp(jnp.sum(lq2_ref[...] * lk2_ref[...]))
           + lam_init)

    for g in range(nq):
        qt = qt_ref[0, :, g * blk:(g + 1) * blk]
        sub = lax.broadcasted_iota(jnp.int32, qt.shape, 0)
        zero = jnp.zeros_like(qt)
        qq_ref[g, :, 0:blk] = jnp.where(sub < DIFF_QK_DIM, qt, zero)
        qq_ref[g, :, blk:2 * blk] = jnp.where(sub >= DIFF_QK_DIM, qt, zero)

    def query_block(j):
        return pl.program_id(2) * nq + j // nblk

    def key_block(j):
        kb = query_block(j) - 1 + j % nblk
        kb = jnp.where(kb < 0, kb + nblk, kb)
        return jnp.where(kb >= nblk, kb - nblk, kb)

    def near(j):
        return j % nblk < NEAR_BLOCKS

    def scores(j):
        k0 = pl.multiple_of(key_block(j) * blk, blk)
        mx8 = jnp.full((8, 2 * blk), -jnp.inf, F32)
        for r in range(nstrip):
            s = _dot(k_ref[0, 0, pl.ds(k0 + r * strip, strip), :], qq_ref[j // nblk])
            s_ref[j % 3, r * strip:(r + 1) * strip, :] = s
            if not near(j):
                for i in range(strip // 8):
                    mx8 = jnp.maximum(mx8, s[i * 8:(i + 1) * 8])
        if not near(j):
            mx_ref[j % 4] = mx8

    def biased(j, r):
        rows = slice(r * strip, (r + 1) * strip)
        s = s_ref[j % 3, rows, :]
        if near(j):
            b = bias_ref[0, jnp.clip(key_block(j) - query_block(j), -2, 2) + 2, rows, :]
            s = s + jnp.concatenate([b, b], axis=1)
        return s

    def stats(j):
        g = j // nblk
        if near(j):
            mx8 = jnp.full((8, 2 * blk), -jnp.inf, F32)
            for r in range(nstrip):
                s = biased(j, r)
                for i in range(strip // 8):
                    mx8 = jnp.maximum(mx8, s[i * 8:(i + 1) * 8])
        else:
            mx8 = mx_ref[j % 4]
        shift = 0.0 if near(j) else jnp.where(key_block(j) < query_block(j), far_left, far_right)
        m_blk = jnp.max(mx8, axis=0, keepdims=True) + shift
        if j % nblk == 0:
            m_new = m_blk
        else:
            m_prev = m_ref[g]
            m_new = jnp.maximum(m_prev, m_blk)
            st_ref[j % 4, 0:1, :] = jnp.exp2(m_prev - m_new)
        m_ref[g] = m_new
        st_ref[j % 4, 1:2, :] = m_new - shift

    def probs(j):
        off = st_ref[j % 4, 1:2, :]
        for r in range(nstrip):
            p_ref[j % 2, r * strip:(r + 1) * strip, :] = jnp.exp2(biased(j, r) - off).astype(BF16)

    def values(j):
        g = j // nblk
        k0 = pl.multiple_of(key_block(j) * blk, blk)
        vta = jnp.concatenate([vt_ref[0, :, pl.ds(k0, blk)], ones], axis=0)
        pv = _dot(vta, p_ref[j % 2])
        acc_ref[g] = pv if j % nblk == 0 else st_ref[j % 4, 0:1, :] * acc_ref[g] + pv
        if j % nblk == nblk - 1:
            acc = acc_ref[g]
            ot = acc[:HEAD_DIM] / acc[HEAD_DIM:HEAD_DIM + 1]
            o = (ot[:, :blk] - lam * ot[:, blk:]).T
            o_ref[0, g * blk:(g + 1) * blk, :] = (_rms(o, gd_ref[...]) * (1.0 - lam_init)).astype(BF16)

    njobs = nq * nblk
    for stage in range(njobs + 3):
        for step, lag in ((values, 3), (scores, 0), (stats, 1), (probs, 2)):
            if 0 <= stage - lag < njobs:
                step(stage - lag)


def _diff_attn(rel_bias, qt, k, vt, bias, lq1, lk1, lq2, lk2, g_diff, nh, lam_init):
    b, _, t, _ = k.shape
    blk = ATT_BLOCK
    nq = ATT_QBLOCKS
    assert t // blk >= NEAR_BLOCKS and (t // blk) % nq == 0
    small = lambda h, bi, qi: (0, 0)
    return pl.pallas_call(
        functools.partial(_attn_kernel, blk=blk, strip=ATT_STRIP, seq=t, nq=nq, lam_init=lam_init),
        grid=(nh, b, t // (nq * blk)),
        in_specs=[pl.BlockSpec(memory_space=pltpu.SMEM),
                  pl.BlockSpec((1, HEAD_DIM, nq * blk), lambda h, bi, qi: (bi, h, qi)),
                  pl.BlockSpec((1, 1, t, HEAD_DIM), lambda h, bi, qi: (bi, h, 0, 0)),
                  pl.BlockSpec((1, HEAD_DIM, t), lambda h, bi, qi: (bi, h, 0)),
                  pl.BlockSpec((1, 5, blk, blk), lambda h, bi, qi: (h, 0, 0, 0)),
                  pl.BlockSpec((1, DIFF_QK_DIM), small), pl.BlockSpec((1, DIFF_QK_DIM), small),
                  pl.BlockSpec((1, DIFF_QK_DIM), small), pl.BlockSpec((1, DIFF_QK_DIM), small),
                  pl.BlockSpec((1, HEAD_DIM), small)],
        out_specs=pl.BlockSpec((1, nq * blk, HEAD_DIM), lambda h, bi, qi: (bi, qi, h)),
        out_shape=jax.ShapeDtypeStruct((b, t, nh * HEAD_DIM), BF16),
        scratch_shapes=[pltpu.VMEM((nq, HEAD_DIM, 2 * blk), BF16),
                        pltpu.VMEM((nq, 1, 2 * blk), F32),
                        pltpu.VMEM((4, 8, 2 * blk), F32),
                        pltpu.VMEM((4, 2, 2 * blk), F32),
                        pltpu.VMEM((nq, HEAD_DIM + ONES_ROWS, 2 * blk), F32),
                        pltpu.VMEM((3, blk, 2 * blk), F32),
                        pltpu.VMEM((2, blk, 2 * blk), BF16)],
        compiler_params=_cparams(("parallel", "parallel", "arbitrary")),
        name="diff_attn",
    )(rel_bias, qt, k, vt, bias, lq1, lk1, lq2, lk2, g_diff)


def _hgrn_masks(rev):
    c = HGRN_CHUNK
    row = lax.broadcasted_iota(jnp.int32, (c, c), 0)
    col = lax.broadcasted_iota(jnp.int32, (c, c), 1)
    tri = (col >= row) if rev else (col <= row)
    masks = []
    b = HGRN_BASE
    while b < c:
        masks.append((b, ((row ^ col) >> int(math.log2(b))) == 1))
        b *= 2
    sh = int(math.log2(HGRN_BASE))
    diag = ((row >> sh) == (col >> sh)) & tri
    return tri.astype(BF16), masks, diag


def _row_refs(cum, idxs, rows_each):
    return jnp.concatenate(
        [jnp.broadcast_to(cum[i:i + 1, :], (rows_each, cum.shape[1])) for i in idxs], axis=0)


def _hgrn_chunks(q, kk, logf, vs, vts, st, tri, masks, diag, rev):
    c = HGRN_CHUNK
    n = len(vs)
    d = q.shape[1] // n
    part = lambda x, i: x[:, i * d:(i + 1) * d]
    top = lax.bitcast_convert_type(lax.bitcast_convert_type(logf, jnp.int32) & TOP16, F32)
    cum = _dot(tri, top.astype(BF16)) + _dot(tri, (logf - top).astype(BF16))
    last = cum[0:1, :] if rev else cum[c - 1:c, :]
    rowi = lax.broadcasted_iota(jnp.int32, q.shape, 0)

    a = [jnp.zeros((c, c), F32)] * n
    for b, siblings in masks:
        ngroups = c // (2 * b)
        ref = _row_refs(cum, [g * 2 * b + (b if rev else b - 1) for g in range(ngroups)], 2 * b)
        first_half = (rowi & (2 * b - 1)) < b
        is_q = first_half if rev else jnp.logical_not(first_half)
        qt = jnp.where(is_q, q * jnp.exp(jnp.minimum(cum - ref, 0.0)), 0.0).astype(BF16)
        kt = jnp.where(is_q, 0.0, kk * jnp.exp(jnp.minimum(ref - cum, 0.0))).astype(BF16)
        a = [jnp.where(siblings, _dot_nt(part(qt, i), part(kt, i)), a[i]) for i in range(n)]
    nb = c // HGRN_BASE
    ref = _row_refs(cum, [g * HGRN_BASE + HGRN_BASE // 2 for g in range(nb)], HGRN_BASE)
    qd = (q * jnp.exp(cum - ref)).astype(BF16)
    kd = (kk * jnp.exp(ref - cum)).astype(BF16)
    a = [jnp.where(diag, _dot_nt(part(qd, i), part(kd, i)), a[i]) for i in range(n)]

    qe = (q * jnp.exp(cum)).astype(BF16)
    kl = (kk * jnp.exp(last - cum)).astype(BF16)
    decay = jnp.exp(last)
    intra = [_dot(a[i].astype(BF16), vs[i]) for i in range(n)]
    update = [_dot(vts[i], part(kl, i)) for i in range(n)]
    outs = [None] * n
    for i in (range(n - 1, -1, -1) if rev else range(n)):
        outs[i] = intra[i] + _dot_nt(part(qe, i), st.astype(BF16))
        st = st * part(decay, i) + update[i]
    return outs, st


def _hgrn_kernel(*refs, rev, nchunk, level):
    if rev:
        q_ref, v_ref, vt_ref, f_ref, lb_ref, of_ref, g_ref, gn_ref, o_ref, st_ref = refs
    else:
        q_ref, v_ref, vt_ref, f_ref, lb_ref, o_ref, st_ref = refs

    @pl.when(pl.program_id(2) == 0)
    def _():
        st_ref[...] = jnp.zeros(st_ref.shape, F32)

    tri, masks, diag = _hgrn_masks(rev)
    c = HGRN_CHUNK
    d = HEAD_DIM
    chunks = [slice(ci * c, (ci + 1) * c) for ci in range(nchunk)]
    for hh in range(q_ref.shape[2] // d):
        head = slice(hh * d, (hh + 1) * d)
        lp = lb_ref[0, :, head]
        e = jnp.exp(lp - jnp.max(lp, axis=0, keepdims=True))
        lb = jnp.sum(e[0:level + 1], axis=0, keepdims=True) / jnp.sum(e, axis=0, keepdims=True)

        wide = lambda ref: jnp.concatenate([ref[0, rows, head] for rows in chunks], axis=1)
        lbw = jnp.concatenate([lb] * nchunk, axis=1)
        f = lbw + (1.0 - lbw) * _sigmoid(wide(f_ref))
        outs, st = _hgrn_chunks(wide(q_ref).astype(F32), 1.0 - f, jnp.log(f),
                                [v_ref[0, rows, head] for rows in chunks],
                                [vt_ref[0, head, rows] for rows in chunks],
                                st_ref[hh], tri, masks, diag, rev)
        st_ref[hh] = st
        for rows, out in zip(chunks, outs):
            if rev:
                o = out + of_ref[0, rows, head]
                o_ref[0, rows, head] = (_rms(o, gn_ref[...]) * g_ref[0, rows, head]).astype(BF16)
            else:
                o_ref[0, rows, head] = out


def _hgrn_dir(hq, hi, hit, hf, lb_param, nh, rev, level, extra=()):
    b, t, _ = hq.shape
    step = min(HGRN_STEP, t)
    nstep = t // step
    hp = HGRN_HEADS
    d = hp * HEAD_DIM
    assert nh % hp == 0
    tpos = (lambda n: nstep - 1 - n) if rev else (lambda n: n)
    row = lambda bi, h, n: (bi, tpos(n), h)
    in_specs = [pl.BlockSpec((1, step, d), row),
                pl.BlockSpec((1, step, d), row),
                pl.BlockSpec((1, d, step), lambda bi, h, n: (bi, h, tpos(n))),
                pl.BlockSpec((1, step, d), lambda bi, h, n: (bi, tpos(n), (nh // hp if rev else 0) + h)),
                pl.BlockSpec((1, lb_param.shape[1], d), lambda bi, h, n: (1 if rev else 0, 0, h))]
    if rev:
        in_specs += [pl.BlockSpec((1, step, d), row), pl.BlockSpec((1, step, d), row),
                     pl.BlockSpec((1, HEAD_DIM), lambda bi, h, n: (0, 0))]
    return pl.pallas_call(
        functools.partial(_hgrn_kernel, rev=rev, nchunk=step // HGRN_CHUNK, level=level),
        grid=(b, nh // hp, nstep),
        in_specs=in_specs,
        out_specs=pl.BlockSpec((1, step, d), row),
        out_shape=jax.ShapeDtypeStruct((b, t, nh * HEAD_DIM), BF16 if rev else F32),
        scratch_shapes=[pltpu.VMEM((hp, HEAD_DIM, HEAD_DIM), F32)],
        compiler_params=_cparams(("parallel", "parallel", "arbitrary")),
        name="hgrn_bwd" if rev else "hgrn_fwd",
    )(hq, hi, hit, hf, lb_param, *extra)


def _out_proj_kernel(a_ref, o_ref, x_ref, w_ref, gp_ref, gf_ref, h_ref, u_ref, *, wd):
    m = _dot(a_ref[...], w_ref[0:wd, :]) + _dot(o_ref[...], w_ref[wd:, :])
    h = x_ref[...] + _rms(m, gp_ref[...])
    h_ref[...] = h
    u_ref[...] = _rms(h, gf_ref[...]).astype(BF16)


def _out_proj(a, o, x, w, g_post, g_ffn):
    n, d = x.shape
    wd = a.shape[1]
    tm = WIDE_TOKEN_TILE
    row = lambda i: (i, 0)
    fixed = lambda i: (0, 0)
    return pl.pallas_call(
        functools.partial(_out_proj_kernel, wd=wd),
        grid=(n // tm,),
        in_specs=[pl.BlockSpec((tm, wd), row), pl.BlockSpec((tm, o.shape[1]), row), pl.BlockSpec((tm, d), row),
                  pl.BlockSpec(w.shape, fixed), pl.BlockSpec((1, d), fixed), pl.BlockSpec((1, d), fixed)],
        out_specs=[pl.BlockSpec((tm, d), row), pl.BlockSpec((tm, d), row)],
        out_shape=[jax.ShapeDtypeStruct((n, d), F32), jax.ShapeDtypeStruct((n, d), BF16)],
        compiler_params=_cparams(("parallel",)),
        name="out_proj",
    )(a, o, x, w, g_post, g_ffn)


def _ffn_kernel(u_ref, up_ref, un_ref, wg_ref, wu_ref, cwg_ref, cwu_ref, cbg_ref, cbu_ref, wd_ref,
                h_ref, g_ref, o_ref, acc_ref, sh_ref, *, tiles_per_seq, sub):
    i = pl.program_id(0)
    j = pl.program_id(1)
    nj = pl.num_programs(1)
    tm = u_ref.shape[0]

    @pl.when(j == 0)
    def _():
        acc_ref[...] = jnp.zeros(acc_ref.shape, F32)

    u = u_ref[...]
    halo = jnp.concatenate([up_ref[...], un_ref[...]], axis=0)
    ti = i % tiles_per_seq
    has_prev = jnp.where(ti > 0, 1.0, 0.0)
    has_next = jnp.where(ti < tiles_per_seq - 1, 1.0, 0.0)
    c0 = math.sqrt(2.0 / math.pi)
    c1 = c0 * 0.044715
    lane_tiles = sub // LANES
    pad = SUBLANES

    def slab(c, branch, lt):
        return ((c % FFN_LOOKAHEAD_SETS) * 2 + branch) * lane_tiles + lt

    def project(c):
        cols = slice(c * sub, (c + 1) * sub)
        for branch, w_ref in enumerate((wg_ref, wu_ref)):
            w = w_ref[:, cols]
            a = _dot(u, w)
            ah = _dot(halo, w)
            for lt in range(lane_tiles):
                lanes = slice(lt * LANES, (lt + 1) * LANES)
                sh_ref[slab(c, branch, lt), pad - 1:pad, :] = ah[HALO_ROWS - 1:HALO_ROWS, lanes] * has_prev
                sh_ref[slab(c, branch, lt), pad:pad + tm, :] = a[:, lanes]
                sh_ref[slab(c, branch, lt), pad + tm:pad + tm + 1, :] = ah[HALO_ROWS:HALO_ROWS + 1, lanes] * has_next

    def conv(c, branch, cw_ref, cb_ref):
        outs = []
        for lt in range(lane_tiles):
            s = slab(c, branch, lt)
            cols = slice(c * sub + lt * LANES, c * sub + (lt + 1) * LANES)
            cw = cw_ref[:, cols]
            outs.append(cw[0:1, :] * sh_ref[s, pad - 1:pad - 1 + tm, :] + cw[1:2, :] * sh_ref[s, pad:pad + tm, :]
                        + cw[2:3, :] * sh_ref[s, pad + 1:pad + 1 + tm, :] + cb_ref[:, cols])
        return jnp.concatenate(outs, axis=1)

    def activate(c):
        gate = conv(c, 0, cwg_ref, cbg_ref)
        up = conv(c, 1, cwu_ref, cbu_ref)
        return (gate * up * (1.0 + jnp.tanh(gate * (c0 + c1 * (gate * gate))))).astype(BF16)

    nsub = wg_ref.shape[1] // sub
    for c in range(min(FFN_LOOKAHEAD, nsub)):
        project(c)
    for c in range(nsub):
        if c + FFN_LOOKAHEAD < nsub:
            project(c + FFN_LOOKAHEAD)
        acc_ref[...] += _dot(activate(c), wd_ref[c * sub:(c + 1) * sub, :])

    @pl.when(j == nj - 1)
    def _():
        o_ref[...] = h_ref[...] + _rms(acc_ref[...], g_ref[...])


def _conv_ffn(u, h, w_up, conv_w, conv_b, w_down, g_post, seq):
    n, d = h.shape
    dff = w_down.shape[0]
    tm = min(FFN_TOKEN_TILE, seq)
    tf = FFN_COL_TILE
    nf = dff // tf
    hb = tm // HALO_ROWS
    nhalo = n // HALO_ROWS
    row = lambda i, j: (i, 0)
    return pl.pallas_call(
        functools.partial(_ffn_kernel, tiles_per_seq=seq // tm, sub=FFN_SUB_TILE),
        grid=(n // tm, nf),
        in_specs=[pl.BlockSpec((tm, d), row),
                  pl.BlockSpec((HALO_ROWS, d), lambda i, j: (jnp.maximum(i * hb - 1, 0), 0)),
                  pl.BlockSpec((HALO_ROWS, d), lambda i, j: (jnp.minimum((i + 1) * hb, nhalo - 1), 0)),
                  pl.BlockSpec((d, tf), lambda i, j: (0, j)),
                  pl.BlockSpec((d, tf), lambda i, j: (0, nf + j)),
                  pl.BlockSpec((CONV_WIDTH, tf), lambda i, j: (0, j)),
                  pl.BlockSpec((CONV_WIDTH, tf), lambda i, j: (0, nf + j)),
                  pl.BlockSpec((1, tf), lambda i, j: (0, j)),
                  pl.BlockSpec((1, tf), lambda i, j: (0, nf + j)),
                  pl.BlockSpec((tf, d), lambda i, j: (j, 0)),
                  pl.BlockSpec((tm, d), row),
                  pl.BlockSpec((1, d), lambda i, j: (0, 0))],
        out_specs=pl.BlockSpec((tm, d), row),
        out_shape=jax.ShapeDtypeStruct((n, d), F32),
        scratch_shapes=[pltpu.VMEM((tm, d), F32),
                        pltpu.VMEM((FFN_LOOKAHEAD_SETS * 2 * FFN_SUB_TILE // LANES, tm + 2 * SUBLANES, LANES), F32)],
        compiler_params=_cparams(("parallel", "arbitrary")),
        name="conv_ffn",
    )(u, u, u, w_up, w_up, conv_w, conv_w, conv_b, conv_b, w_down, h, g_post)


def _ple_kernel(h_ref, p_ref, wp_ref, wg_ref, g_ref, o_ref):
    h = h_ref[...]
    e = _rms(_dot(p_ref[...].astype(BF16), wp_ref[...]), g_ref[...])
    gate = _sigmoid(_dot(h.astype(BF16), wg_ref[...]))
    o_ref[...] = h + e * gate


def _ple(h, p, w_ple, w_gate, g):
    n, d = h.shape
    tm = WIDE_TOKEN_TILE
    row = lambda i: (i, 0)
    fixed = lambda i: (0, 0)
    return pl.pallas_call(
        _ple_kernel,
        grid=(n // tm,),
        in_specs=[pl.BlockSpec((tm, d), row), pl.BlockSpec((tm, p.shape[1]), row),
                  pl.BlockSpec(w_ple.shape, fixed), pl.BlockSpec(w_gate.shape, fixed), pl.BlockSpec((1, d), fixed)],
        out_specs=pl.BlockSpec((tm, d), row),
        out_shape=jax.ShapeDtypeStruct((n, d), F32),
        compiler_params=_cparams(("parallel",)),
        name="ple",
    )(h, p, w_ple, w_gate, g)


def kernel(x, p, rel_bias, g_pre_mix, w_in, lambda_q1, lambda_k1, lambda_q2, lambda_k2, g_diff, lb_param, g_hgrn, w_out, g_post_mix, g_pre_ffn, w_up, conv_w, conv_b, w_down, g_post_ffn, w_ple, g_ple, w_ple_gate):
    b, t, d = x.shape
    depth = w_in.shape[0]
    wd = d // 2
    wh = d - wd
    nh_diff = wd // HEAD_DIM
    nh_hgrn = wh // HEAD_DIM
    assert w_in.shape[2] == 3 * wd + 5 * wh
    assert t % ATT_BLOCK == 0 and t % HGRN_CHUNK == 0 and (b * t) % WIDE_TOKEN_TILE == 0

    rb = rel_bias.astype(F32)
    bias = _rel_bias_blocks(rb, ATT_BLOCK)
    h = x
    for l in range(depth):
        vec = lambda a: a[l].reshape(1, -1).astype(F32)
        hid = h
        qt, k, vt, hq, hi, hit, hf, hg = _in_proj(hid, vec(g_pre_mix), w_in[l].astype(BF16), wd, wh)
        lam_init = 0.8 - 0.6 * math.exp(-0.3 * l)
        a = _diff_attn(rb, qt, k, vt, bias, vec(lambda_q1), vec(lambda_k1), vec(lambda_q2), vec(lambda_k2),
                       vec(g_diff), nh_diff, lam_init)
        lbp = lb_param.astype(F32)
        o_fwd = _hgrn_dir(hq, hi, hit, hf, lbp, nh_hgrn, False, l)
        o = _hgrn_dir(hq, hi, hit, hf, lbp, nh_hgrn, True, l, extra=(o_fwd, hg, vec(g_hgrn)))
        n = b * t
        h1, u2 = _out_proj(a.reshape(n, wd), o.reshape(n, wh), hid.reshape(n, d), w_out[l].astype(BF16),
                           vec(g_post_mix), vec(g_pre_ffn))
        h2 = _conv_ffn(u2, h1, w_up[l].astype(BF16), conv_w[l].astype(F32), conv_b[l].reshape(1, -1).astype(F32),
                       (0.5 * w_down[l]).astype(BF16), vec(g_post_ffn), t)
        h3 = _ple(h2, p[l].reshape(n, -1), w_ple[l].astype(BF16), w_ple_gate[l].astype(BF16), vec(g_ple))
        h = h3.reshape(b, t, d)
    return h
```

```python
import functools
import math

import jax
import jax.numpy as jnp
from jax import lax
from jax.experimental import pallas as pl
from jax.experimental.pallas import tpu as pltpu

F32 = jnp.float32
BF16 = jnp.bfloat16

EPS = 1e-6
SUBLANES = 8
LANES = 128
DIFF_QK_DIM = 64
HEAD_DIM = 128
REL_BUCKETS = 32
REL_MAX_DIST = 128
CONV_WIDTH = 3

ATT_BLOCK = 512
ATT_STRIP = 512
ATT_QBLOCKS = 2
NEAR_BLOCKS = 3
ONES_ROWS = 16
LOG2E = math.log2(math.e)
TOP16 = -65536
HGRN_CHUNK = 128
HGRN_BASE = 16
HGRN_STEP = 2048
HGRN_HEADS = 4
TOKEN_TILE = 512
WIDE_TOKEN_TILE = 1024
FFN_TOKEN_TILE = 1024
FFN_COL_TILE = 1024
FFN_SUB_TILE = 256
FFN_LOOKAHEAD = 3
FFN_LOOKAHEAD_SETS = 4
HALO_ROWS = 16
VMEM_LIMIT = 56 * 1024 * 1024


def _cparams(sem, flags=None):
    return pltpu.CompilerParams(dimension_semantics=sem, vmem_limit_bytes=VMEM_LIMIT, flags=flags)


def _rms(x, g):
    return x * lax.rsqrt(jnp.mean(x * x, axis=-1, keepdims=True) + EPS) * g


def _sigmoid(x):
    return 1.0 / (1.0 + jnp.exp(-x))


def _dot(a, b):
    return jnp.dot(a, b, preferred_element_type=F32)


def _dot_nt(a, b):
    return lax.dot_general(a, b, (((1,), (1,)), ((), ())), preferred_element_type=F32)


def _log_bucket_thresholds():
    half = REL_BUCKETS // 2
    max_exact = half // 2
    nlog = half - max_exact
    thr = []
    for k in range(1, nlog):
        n = max_exact
        while (n ** nlog) * (max_exact ** k) < (REL_MAX_DIST ** k) * (max_exact ** nlog):
            n += 1
        thr.append(n)
    return max_exact, half, thr


def _bias_kernel(rb_ref, o_ref, *, blk):
    h = pl.program_id(0)
    d = pl.program_id(1) - 2
    max_exact, half, thr = _log_bucket_thresholds()
    width = 2 * blk
    z = lax.broadcasted_iota(jnp.int32, (SUBLANES, width), 1)
    z = jnp.where(z < blk, z, z - width)
    rel = d * blk - z
    n = jnp.abs(rel)
    large = jnp.full(rel.shape, max_exact, jnp.int32)
    for t in thr:
        large = large + (n >= t).astype(jnp.int32)
    bucket = jnp.where(rel > 0, half, 0) + jnp.where(n < max_exact, n, large)
    bias = jnp.zeros(rel.shape, F32)
    for b in range(REL_BUCKETS):
        bias = jnp.where(bucket == b, rb_ref[b, h], bias)
    rows = jnp.broadcast_to(bias[0:1, :] * LOG2E, (blk, width))
    o_ref[0, 0] = pltpu.roll(rows, 0, 1, stride=1, stride_axis=0)[:, :blk]


def _rel_bias_blocks(rel_bias, blk):
    assert blk >= REL_MAX_DIST
    nh = rel_bias.shape[1]
    return pl.pallas_call(
        functools.partial(_bias_kernel, blk=blk),
        grid=(nh, 5),
        in_specs=[pl.BlockSpec(memory_space=pltpu.SMEM)],
        out_specs=pl.BlockSpec((1, 1, blk, blk), lambda h, d: (h, d, 0, 0)),
        out_shape=jax.ShapeDtypeStruct((nh, 5, blk, blk), F32),
        compiler_params=_cparams(("parallel", "parallel")),
        name="rel_bias_blocks",
    )(rel_bias)


def _in_proj_kernel(x_ref, g_ref, w_ref, qt_ref, k_ref, vt_ref, hq_ref, hi_ref, hit_ref, hf_ref, hg_ref,
                    *, wd, wh):
    u = _rms(x_ref[0], g_ref[...]).astype(BF16)

    def proj(a, b):
        return _dot(u, w_ref[:, a:b])

    qt_ref[0] = (proj(0, wd) * (DIFF_QK_DIM ** -0.5 * LOG2E)).T.astype(BF16)
    kp = proj(wd, 2 * wd).astype(BF16)
    for h in range(wd // HEAD_DIM):
        k_ref[0, h] = kp[:, h * HEAD_DIM:(h + 1) * HEAD_DIM]
    vt_ref[0] = proj(2 * wd, 3 * wd).T.astype(BF16)
    o = 3 * wd
    qh = proj(o, o + wh)
    hq_ref[0] = (qh * _sigmoid(qh)).astype(BF16)
    ih = proj(o + wh, o + 2 * wh)
    hi_ref[0] = ih.astype(BF16)
    hit_ref[0] = ih.T.astype(BF16)
    hf_ref[0] = proj(o + 2 * wh, o + 4 * wh)
    gh = proj(o + 4 * wh, o + 5 * wh)
    hg_ref[0] = gh * _sigmoid(gh)


def _in_proj(x, g, w, wd, wh):
    b, t, d = x.shape
    tm = TOKEN_TILE
    ncol = w.shape[1]
    row = lambda bi, ti: (bi, ti, 0)
    col = lambda bi, ti: (bi, 0, ti)
    return pl.pallas_call(
        functools.partial(_in_proj_kernel, wd=wd, wh=wh),
        grid=(b, t // tm),
        in_specs=[pl.BlockSpec((1, tm, d), row),
                  pl.BlockSpec((1, d), lambda bi, ti: (0, 0)),
                  pl.BlockSpec((d, ncol), lambda bi, ti: (0, 0))],
        out_specs=[pl.BlockSpec((1, wd, tm), col),
                   pl.BlockSpec((1, wd // HEAD_DIM, tm, HEAD_DIM), lambda bi, ti: (bi, 0, ti, 0)),
                   pl.BlockSpec((1, wd, tm), col),
                   pl.BlockSpec((1, tm, wh), row),
                   pl.BlockSpec((1, tm, wh), row),
                   pl.BlockSpec((1, wh, tm), col),
                   pl.BlockSpec((1, tm, 2 * wh), row),
                   pl.BlockSpec((1, tm, wh), row)],
        out_shape=[jax.ShapeDtypeStruct((b, wd, t), BF16),
                   jax.ShapeDtypeStruct((b, wd // HEAD_DIM, t, HEAD_DIM), BF16),
                   jax.ShapeDtypeStruct((b, wd, t), BF16),
                   jax.ShapeDtypeStruct((b, t, wh), BF16),
                   jax.ShapeDtypeStruct((b, t, wh), BF16),
                   jax.ShapeDtypeStruct((b, wh, t), BF16),
                   jax.ShapeDtypeStruct((b, t, 2 * wh), F32),
                   jax.ShapeDtypeStruct((b, t, wh), F32)],
        compiler_params=_cparams(("parallel", "parallel")),
        name="in_proj",
    )(x, g, w)


def _attn_kernel(rb_ref, qt_ref, k_ref, vt_ref, bias_ref, lq1_ref, lk1_ref, lq2_ref, lk2_ref, gd_ref,
                 o_ref, qq_ref, m_ref, mx_ref, st_ref, acc_ref, s_ref, p_ref, *, blk, strip, seq, nq, lam_init):
    h = pl.program_id(0)
    nblk = seq // blk
    nstrip = blk // strip
    ones = jnp.ones((ONES_ROWS, blk), BF16)
    half = REL_BUCKETS // 2
    far_left = rb_ref[half - 1, h] * LOG2E
    far_right = rb_ref[2 * half - 1, h] * LOG2E
    lam = (jnp.exp(jnp.sum(lq1_ref[...] * lk1_ref[...])) - jnp.exp(jnp.sum(lq2_ref[...] * lk2_ref[...]))
           + lam_init)

    for g in range(nq):
        qt = qt_ref[0, :, g * blk:(g + 1) * blk]
        sub = lax.broadcasted_iota(jnp.int32, qt.shape, 0)
        zero = jnp.zeros_like(qt)
        qq_ref[g, :, 0:blk] = jnp.where(sub < DIFF_QK_DIM, qt, zero)
        qq_ref[g, :, blk:2 * blk] = jnp.where(sub >= DIFF_QK_DIM, qt, zero)

    def query_block(j):
        return pl.program_id(2) * nq + j // nblk

    def key_block(j):
        kb = query_block(j) - 1 + j % nblk
        kb = jnp.where(kb < 0, kb + nblk, kb)
        return jnp.where(kb >= nblk, kb - nblk, kb)

    def near(j):
        return j % nblk < NEAR_BLOCKS

    def scores(j):
        k0 = pl.multiple_of(key_block(j) * blk, blk)
        mx8 = jnp.full((8, 2 * blk), -jnp.inf, F32)
        for r in range(nstrip):
            s = _dot(k_ref[0, 0, pl.ds(k0 + r * strip, strip), :], qq_ref[j // nblk])
            s_ref[j % 3, r * strip:(r + 1) * strip, :] = s
            if not near(j):
                for i in range(strip // 8):
                    mx8 = jnp.maximum(mx8, s[i * 8:(i + 1) * 8])
        if not near(j):
            mx_ref[j % 4] = mx8

    def biased(j, r):
        rows = slice(r * strip, (r + 1) * strip)
        s = s_ref[j % 3, rows, :]
        if near(j):
            b = bias_ref[0, jnp.clip(key_block(j) - query_block(j), -2, 2) + 2, rows, :]
            s = s + jnp.concatenate([b, b], axis=1)
        return s

    def stats(j):
        g = j // nblk
        if near(j):
            mx8 = jnp.full((8, 2 * blk), -jnp.inf, F32)
            for r in range(nstrip):
                s = biased(j, r)
                for i in range(strip // 8):
                    mx8 = jnp.maximum(mx8, s[i * 8:(i + 1) * 8])
        else:
            mx8 = mx_ref[j % 4]
        shift = 0.0 if near(j) else jnp.where(key_block(j) < query_block(j), far_left, far_right)
        m_blk = jnp.max(mx8, axis=0, keepdims=True) + shift
        if j % nblk == 0:
            m_new = m_blk
        else:
            m_prev = m_ref[g]
            m_new = jnp.maximum(m_prev, m_blk)
            st_ref[j % 4, 0:1, :] = jnp.exp2(m_prev - m_new)
        m_ref[g] = m_new
        st_ref[j % 4, 1:2, :] = m_new - shift

    def probs(j):
        off = st_ref[j % 4, 1:2, :]
        for r in range(nstrip):
            p_ref[j % 2, r * strip:(r + 1) * strip, :] = jnp.exp2(biased(j, r) - off).astype(BF16)

    def values(j):
        g = j // nblk
        k0 = pl.multiple_of(key_block(j) * blk, blk)
        vta = jnp.concatenate([vt_ref[0, :, pl.ds(k0, blk)], ones], axis=0)
        pv = _dot(vta, p_ref[j % 2])
        acc_ref[g] = pv if j % nblk == 0 else st_ref[j % 4, 0:1, :] * acc_ref[g] + pv
        if j % nblk == nblk - 1:
            acc = acc_ref[g]
            ot = acc[:HEAD_DIM] / acc[HEAD_DIM:HEAD_DIM + 1]
            o = (ot[:, :blk] - lam * ot[:, blk:]).T
            o_ref[0, g * blk:(g + 1) * blk, :] = (_rms(o, gd_ref[...]) * (1.0 - lam_init)).astype(BF16)

    njobs = nq * nblk
    for stage in range(njobs + 3):
        for step, lag in ((values, 3), (scores, 0), (stats, 1), (probs, 2)):
            if 0 <= stage - lag < njobs:
                step(stage - lag)


def _diff_attn(rel_bias, qt, k, vt, bias, lq1, lk1, lq2, lk2, g_diff, nh, lam_init):
    b, _, t, _ = k.shape
    blk = ATT_BLOCK
    nq = ATT_QBLOCKS
    assert t // blk >= NEAR_BLOCKS and (t // blk) % nq == 0
    small = lambda h, bi, qi: (0, 0)
    return pl.pallas_call(
        functools.partial(_attn_kernel, blk=blk, strip=ATT_STRIP, seq=t, nq=nq, lam_init=lam_init),
        grid=(nh, b, t // (nq * blk)),
        in_specs=[pl.BlockSpec(memory_space=pltpu.SMEM),
                  pl.BlockSpec((1, HEAD_DIM, nq * blk), lambda h, bi, qi: (bi, h, qi)),
                  pl.BlockSpec((1, 1, t, HEAD_DIM), lambda h, bi, qi: (bi, h, 0, 0)),
                  pl.BlockSpec((1, HEAD_DIM, t), lambda h, bi, qi: (bi, h, 0)),
                  pl.BlockSpec((1, 5, blk, blk), lambda h, bi, qi: (h, 0, 0, 0)),
                  pl.BlockSpec((1, DIFF_QK_DIM), small), pl.BlockSpec((1, DIFF_QK_DIM), small),
                  pl.BlockSpec((1, DIFF_QK_DIM), small), pl.BlockSpec((1, DIFF_QK_DIM), small),
                  pl.BlockSpec((1, HEAD_DIM), small)],
        out_specs=pl.BlockSpec((1, nq * blk, HEAD_DIM), lambda h, bi, qi: (bi, qi, h)),
        out_shape=jax.ShapeDtypeStruct((b, t, nh * HEAD_DIM), BF16),
        scratch_shapes=[pltpu.VMEM((nq, HEAD_DIM, 2 * blk), BF16),
                        pltpu.VMEM((nq, 1, 2 * blk), F32),
                        pltpu.VMEM((4, 8, 2 * blk), F32),
                        pltpu.VMEM((4, 2, 2 * blk), F32),
                        pltpu.VMEM((nq, HEAD_DIM + ONES_ROWS, 2 * blk), F32),
                        pltpu.VMEM((3, blk, 2 * blk), F32),
                        pltpu.VMEM((2, blk, 2 * blk), BF16)],
        compiler_params=_cparams(("parallel", "parallel", "arbitrary")),
        name="diff_attn",
    )(rel_bias, qt, k, vt, bias, lq1, lk1, lq2, lk2, g_diff)


def _hgrn_masks(rev):
    c = HGRN_CHUNK
    row = lax.broadcasted_iota(jnp.int32, (c, c), 0)
    col = lax.broadcasted_iota(jnp.int32, (c, c), 1)
    tri = (col >= row) if rev else (col <= row)
    masks = []
    b = HGRN_BASE
    while b < c:
        masks.append((b, ((row ^ col) >> int(math.log2(b))) == 1))
        b *= 2
    sh = int(math.log2(HGRN_BASE))
    diag = ((row >> sh) == (col >> sh)) & tri
    return tri.astype(BF16), masks, diag


def _row_refs(cum, idxs, rows_each):
    return jnp.concatenate(
        [jnp.broadcast_to(cum[i:i + 1, :], (rows_each, cum.shape[1])) for i in idxs], axis=0)


def _hgrn_chunks(q, kk, logf, vs, vts, st, tri, masks, diag, rev):
    c = HGRN_CHUNK
    n = len(vs)
    d = q.shape[1] // n
    part = lambda x, i: x[:, i * d:(i + 1) * d]
    top = lax.bitcast_convert_type(lax.bitcast_convert_type(logf, jnp.int32) & TOP16, F32)
    cum = _dot(tri, top.astype(BF16)) + _dot(tri, (logf - top).astype(BF16))
    last = cum[0:1, :] if rev else cum[c - 1:c, :]
    rowi = lax.broadcasted_iota(jnp.int32, q.shape, 0)

    a = [jnp.zeros((c, c), F32)] * n
    for b, siblings in masks:
        ngroups = c // (2 * b)
        ref = _row_refs(cum, [g * 2 * b + (b if rev else b - 1) for g in range(ngroups)], 2 * b)
        first_half = (rowi & (2 * b - 1)) < b
        is_q = first_half if rev else jnp.logical_not(first_half)
        qt = jnp.where(is_q, q * jnp.exp(jnp.minimum(cum - ref, 0.0)), 0.0).astype(BF16)
        kt = jnp.where(is_q, 0.0, kk * jnp.exp(jnp.minimum(ref - cum, 0.0))).astype(BF16)
        a = [jnp.where(siblings, _dot_nt(part(qt, i), part(kt, i)), a[i]) for i in range(n)]
    nb = c // HGRN_BASE
    ref = _row_refs(cum, [g * HGRN_BASE + HGRN_BASE // 2 for g in range(nb)], HGRN_BASE)
    qd = (q * jnp.exp(cum - ref)).astype(BF16)
    kd = (kk * jnp.exp(ref - cum)).astype(BF16)
    a = [jnp.where(diag, _dot_nt(part(qd, i), part(kd, i)), a[i]) for i in range(n)]

    qe = (q * jnp.exp(cum)).astype(BF16)
    kl = (kk * jnp.exp(last - cum)).astype(BF16)
    decay = jnp.exp(last)
    intra = [_dot(a[i].astype(BF16), vs[i]) for i in range(n)]
    update = [_dot(vts[i], part(kl, i)) for i in range(n)]
    outs = [None] * n
    for i in (range(n - 1, -1, -1) if rev else range(n)):
        outs[i] = intra[i] + _dot_nt(part(qe, i), st.astype(BF16))
        st = st * part(decay, i) + update[i]
    return outs, st


def _hgrn_kernel(*refs, rev, nchunk, level):
    if rev:
        q_ref, v_ref, vt_ref, f_ref, lb_ref, of_ref, g_ref, gn_ref, o_ref, st_ref = refs
    else:
        q_ref, v_ref, vt_ref, f_ref, lb_ref, o_ref, st_ref = refs

    @pl.when(pl.program_id(2) == 0)
    def _():
        st_ref[...] = jnp.zeros(st_ref.shape, F32)

    tri, masks, diag = _hgrn_masks(rev)
    c = HGRN_CHUNK
    d = HEAD_DIM
    chunks = [slice(ci * c, (ci + 1) * c) for ci in range(nchunk)]
    for hh in range(q_ref.shape[2] // d):
        head = slice(hh * d, (hh + 1) * d)
        lp = lb_ref[0, :, head]
        e = jnp.exp(lp - jnp.max(lp, axis=0, keepdims=True))
        lb = jnp.sum(e[0:level + 1], axis=0, keepdims=True) / jnp.sum(e, axis=0, keepdims=True)

        wide = lambda ref: jnp.concatenate([ref[0, rows, head] for rows in chunks], axis=1)
        lbw = jnp.concatenate([lb] * nchunk, axis=1)
        f = lbw + (1.0 - lbw) * _sigmoid(wide(f_ref))
        outs, st = _hgrn_chunks(wide(q_ref).astype(F32), 1.0 - f, jnp.log(f),
                                [v_ref[0, rows, head] for rows in chunks],
                                [vt_ref[0, head, rows] for rows in chunks],
                                st_ref[hh], tri, masks, diag, rev)
        st_ref[hh] = st
        for rows, out in zip(chunks, outs):
            if rev:
                o = out + of_ref[0, rows, head]
                o_ref[0, rows, head] = (_rms(o, gn_ref[...]) * g_ref[0, rows, head]).astype(BF16)
            else:
                o_ref[0, rows, head] = out


def _hgrn_dir(hq, hi, hit, hf, lb_param, nh, rev, level, extra=()):
    b, t, _ = hq.shape
    step = min(HGRN_STEP, t)
    nstep = t // step
    hp = HGRN_HEADS
    d = hp * HEAD_DIM
    assert nh % hp == 0
    tpos = (lambda n: nstep - 1 - n) if rev else (lambda n: n)
    row = lambda bi, h, n: (bi, tpos(n), h)
    in_specs = [pl.BlockSpec((1, step, d), row),
                pl.BlockSpec((1, step, d), row),
                pl.BlockSpec((1, d, step), lambda bi, h, n: (bi, h, tpos(n))),
                pl.BlockSpec((1, step, d), lambda bi, h, n: (bi, tpos(n), (nh // hp if rev else 0) + h)),
                pl.BlockSpec((1, lb_param.shape[1], d), lambda bi, h, n: (1 if rev else 0, 0, h))]
    if rev:
        in_specs += [pl.BlockSpec((1, step, d), row), pl.BlockSpec((1, step, d), row),
                     pl.BlockSpec((1, HEAD_DIM), lambda bi, h, n: (0, 0))]
    return pl.pallas_call(
        functools.partial(_hgrn_kernel, rev=rev, nchunk=step // HGRN_CHUNK, level=level),
        grid=(b, nh // hp, nstep),
        in_specs=in_specs,
        out_specs=pl.BlockSpec((1, step, d), row),
        out_shape=jax.ShapeDtypeStruct((b, t, nh * HEAD_DIM), BF16 if rev else F32),
        scratch_shapes=[pltpu.VMEM((hp, HEAD_DIM, HEAD_DIM), F32)],
        compiler_params=_cparams(("parallel", "parallel", "arbitrary")),
        name="hgrn_bwd" if rev else "hgrn_fwd",
    )(hq, hi, hit, hf, lb_param, *extra)


def _out_proj_kernel(a_ref, o_ref, x_ref, w_ref, gp_ref, gf_ref, h_ref, u_ref, *, wd):
    m = _dot(a_ref[...], w_ref[0:wd, :]) + _dot(o_ref[...], w_ref[wd:, :])
    h = x_ref[...] + _rms(m, gp_ref[...])
    h_ref[...] = h
    u_ref[...] = _rms(h, gf_ref[...]).astype(BF16)


def _out_proj(a, o, x, w, g_post, g_ffn):
    n, d = x.shape
    wd = a.shape[1]
    tm = WIDE_TOKEN_TILE
    row = lambda i: (i, 0)
    fixed = lambda i: (0, 0)
    return pl.pallas_call(
        functools.partial(_out_proj_kernel, wd=wd),
        grid=(n // tm,),
        in_specs=[pl.BlockSpec((tm, wd), row), pl.BlockSpec((tm, o.shape[1]), row), pl.BlockSpec((tm, d), row),
                  pl.BlockSpec(w.shape, fixed), pl.BlockSpec((1, d), fixed), pl.BlockSpec((1, d), fixed)],
        out_specs=[pl.BlockSpec((tm, d), row), pl.BlockSpec((tm, d), row)],
        out_shape=[jax.ShapeDtypeStruct((n, d), F32), jax.ShapeDtypeStruct((n, d), BF16)],
        compiler_params=_cparams(("parallel",)),
        name="out_proj",
    )(a, o, x, w, g_post, g_ffn)


def _ffn_kernel(u_ref, up_ref, un_ref, wg_ref, wu_ref, cwg_ref, cwu_ref, cbg_ref, cbu_ref, wd_ref,
                h_ref, g_ref, o_ref, acc_ref, sh_ref, *, tiles_per_seq, sub):
    i = pl.program_id(0)
    j = pl.program_id(1)
    nj = pl.num_programs(1)
    tm = u_ref.shape[0]

    @pl.when(j == 0)
    def _():
        acc_ref[...] = jnp.zeros(acc_ref.shape, F32)

    u = u_ref[...]
    halo = jnp.concatenate([up_ref[...], un_ref[...]], axis=0)
    ti = i % tiles_per_seq
    has_prev = jnp.where(ti > 0, 1.0, 0.0)
    has_next = jnp.where(ti < tiles_per_seq - 1, 1.0, 0.0)
    c0 = math.sqrt(2.0 / math.pi)
    c1 = c0 * 0.044715
    lane_tiles = sub // LANES
    pad = SUBLANES

    def slab(c, branch, lt):
        return ((c % FFN_LOOKAHEAD_SETS) * 2 + branch) * lane_tiles + lt

    def project(c):
        cols = slice(c * sub, (c + 1) * sub)
        for branch, w_ref in enumerate((wg_ref, wu_ref)):
            w = w_ref[:, cols]
            a = _dot(u, w)
            ah = _dot(halo, w)
            for lt in range(lane_tiles):
                lanes = slice(lt * LANES, (lt + 1) * LANES)
                sh_ref[slab(c, branch, lt), pad - 1:pad, :] = ah[HALO_ROWS - 1:HALO_ROWS, lanes] * has_prev
                sh_ref[slab(c, branch, lt), pad:pad + tm, :] = a[:, lanes]
                sh_ref[slab(c, branch, lt), pad + tm:pad + tm + 1, :] = ah[HALO_ROWS:HALO_ROWS + 1, lanes] * has_next

    def conv(c, branch, cw_ref, cb_ref):
        outs = []
        for lt in range(lane_tiles):
            s = slab(c, branch, lt)
            cols = slice(c * sub + lt * LANES, c * sub + (lt + 1) * LANES)
            cw = cw_ref[:, cols]
            outs.append(cw[0:1, :] * sh_ref[s, pad - 1:pad - 1 + tm, :] + cw[1:2, :] * sh_ref[s, pad:pad + tm, :]
                        + cw[2:3, :] * sh_ref[s, pad + 1:pad + 1 + tm, :] + cb_ref[:, cols])
        return jnp.concatenate(outs, axis=1)

    def activate(c):
        gate = conv(c, 0, cwg_ref, cbg_ref)
        up = conv(c, 1, cwu_ref, cbu_ref)
        return (gate * up * (1.0 + jnp.tanh(gate * (c0 + c1 * (gate * gate))))).astype(BF16)

    nsub = wg_ref.shape[1] // sub
    for c in range(min(FFN_LOOKAHEAD, nsub)):
        project(c)
    for c in range(nsub):
        if c + FFN_LOOKAHEAD < nsub:
            project(c + FFN_LOOKAHEAD)
        acc_ref[...] += _dot(activate(c), wd_ref[c * sub:(c + 1) * sub, :])

    @pl.when(j == nj - 1)
    def _():
        o_ref[...] = h_ref[...] + _rms(acc_ref[...], g_ref[...])


def _conv_ffn(u, h, w_up, conv_w, conv_b, w_down, g_post, seq):
    n, d = h.shape
    dff = w_down.shape[0]
    tm = min(FFN_TOKEN_TILE, seq)
    tf = FFN_COL_TILE
    nf = dff // tf
    hb = tm // HALO_ROWS
    nhalo = n // HALO_ROWS
    row = lambda i, j: (i, 0)
    return pl.pallas_call(
        functools.partial(_ffn_kernel, tiles_per_seq=seq // tm, sub=FFN_SUB_TILE),
        grid=(n // tm, nf),
        in_specs=[pl.BlockSpec((tm, d), row),
                  pl.BlockSpec((HALO_ROWS, d), lambda i, j: (jnp.maximum(i * hb - 1, 0), 0)),
                  pl.BlockSpec((HALO_ROWS, d), lambda i, j: (jnp.minimum((i + 1) * hb, nhalo - 1), 0)),
                  pl.BlockSpec((d, tf), lambda i, j: (0, j)),
                  pl.BlockSpec((d, tf), lambda i, j: (0, nf + j)),
                  pl.BlockSpec((CONV_WIDTH, tf), lambda i, j: (0, j)),
                  pl.BlockSpec((CONV_WIDTH, tf), lambda i, j: (0, nf + j)),
                  pl.BlockSpec((1, tf), lambda i, j: (0, j)),
                  pl.BlockSpec((1, tf), lambda i, j: (0, nf + j)),
                  pl.BlockSpec((tf, d), lambda i, j: (j, 0)),
                  pl.BlockSpec((tm, d), row),
                  pl.BlockSpec((1, d), lambda i, j: (0, 0))],
        out_specs=pl.BlockSpec((tm, d), row),
        out_shape=jax.ShapeDtypeStruct((n, d), F32),
        scratch_shapes=[pltpu.VMEM((tm, d), F32),
                        pltpu.VMEM((FFN_LOOKAHEAD_SETS * 2 * FFN_SUB_TILE // LANES, tm + 2 * SUBLANES, LANES), F32)],
        compiler_params=_cparams(("parallel", "arbitrary")),
        name="conv_ffn",
    )(u, u, u, w_up, w_up, conv_w, conv_w, conv_b, conv_b, w_down, h, g_post)


def _ple_kernel(h_ref, p_ref, wp_ref, wg_ref, g_ref, o_ref):
    h = h_ref[...]
    e = _rms(_dot(p_ref[...].astype(BF16), wp_ref[...]), g_ref[...])
    gate = _sigmoid(_dot(h.astype(BF16), wg_ref[...]))
    o_ref[...] = h + e * gate


def _ple(h, p, w_ple, w_gate, g):
    n, d = h.shape
    tm = WIDE_TOKEN_TILE
    row = lambda i: (i, 0)
    fixed = lambda i: (0, 0)
    return pl.pallas_call(
        _ple_kernel,
        grid=(n // tm,),
        in_specs=[pl.BlockSpec((tm, d), row), pl.BlockSpec((tm, p.shape[1]), row),
                  pl.BlockSpec(w_ple.shape, fixed), pl.BlockSpec(w_gate.shape, fixed), pl.BlockSpec((1, d), fixed)],
        out_specs=pl.BlockSpec((tm, d), row),
        out_shape=jax.ShapeDtypeStruct((n, d), F32),
        compiler_params=_cparams(("parallel",)),
        name="ple",
    )(h, p, w_ple, w_gate, g)


def kernel(x, p, rel_bias, g_pre_mix, w_in, lambda_q1, lambda_k1, lambda_q2, lambda_k2, g_diff, lb_param, g_hgrn, w_out, g_post_mix, g_pre_ffn, w_up, conv_w, conv_b, w_down, g_post_ffn, w_ple, g_ple, w_ple_gate):
    b, t, d = x.shape
    depth = w_in.shape[0]
    wd = d // 2
    wh = d - wd
    nh_diff = wd // HEAD_DIM
    nh_hgrn = wh // HEAD_DIM
    assert w_in.shape[2] == 3 * wd + 5 * wh
    assert t % ATT_BLOCK == 0 and t % HGRN_CHUNK == 0 and (b * t) % WIDE_TOKEN_TILE == 0

    rb = rel_bias.astype(F32)
    bias = _rel_bias_blocks(rb, ATT_BLOCK)
    h = x
    for l in range(depth):
        vec = lambda a: a[l].reshape(1, -1).astype(F32)
        hid = h
        qt, k, vt, hq, hi, hit, hf, hg = _in_proj(hid, vec(g_pre_mix), w_in[l].astype(BF16), wd, wh)
        lam_init = 0.8 - 0.6 * math.exp(-0.3 * l)
        a = _diff_attn(rb, qt, k, vt, bias, vec(lambda_q1), vec(lambda_k1), vec(lambda_q2), vec(lambda_k2),
                       vec(g_diff), nh_diff, lam_init)
        lbp = lb_param.astype(F32)
        o_fwd = _hgrn_dir(hq, hi, hit, hf, lbp, nh_hgrn, False, l)
        o = _hgrn_dir(hq, hi, hit, hf, lbp, nh_hgrn, True, l, extra=(o_fwd, hg, vec(g_hgrn)))
        n = b * t
        h1, u2 = _out_proj(a.reshape(n, wd), o.reshape(n, wh), hid.reshape(n, d), w_out[l].astype(BF16),
                           vec(g_post_mix), vec(g_pre_ffn))
        h2 = _conv_ffn(u2, h1, w_up[l].astype(BF16), conv_w[l].astype(F32), conv_b[l].reshape(1, -1).astype(F32),
                       (0.5 * w_down[l]).astype(BF16), vec(g_post_ffn), t)
        h3 = _ple(h2, p[l].reshape(n, -1), w_ple[l].astype(BF16), w_ple_gate[l].astype(BF16), vec(g_ple))
        h = h3.reshape(b, t, d)
    return h
```
